```python
import math
import jax, jax.numpy as jnp
from jax import lax
import numpy as np

D_MODEL = 1024
BATCH = 8
SEQ = 4096
DEPTH = 1

GRID_W = 64
CTX_LEN = 256
HEAD_DIM = 64
ROPE_THETA = 10000.0
NORM_EPS = 1e-6
Q_BLOCK = 128
DIFF_HEADS = 4
DIFF_V_DIM = 2 * HEAD_DIM
DIFF_WIDTH = DIFF_HEADS * DIFF_V_DIM
GQA_Q_HEADS = 8
GQA_KV_HEADS = 2
GQA_GROUP = GQA_Q_HEADS // GQA_KV_HEADS
GQA_WIDTH = GQA_Q_HEADS * HEAD_DIM
MIX_WIDTH = DIFF_WIDTH + GQA_WIDTH
GQA_KV_WIDTH = GQA_KV_HEADS * HEAD_DIM
IN_COLS = 3 * DIFF_WIDTH + GQA_WIDTH + 2 * GQA_KV_WIDTH
_IN_SPLITS = (DIFF_WIDTH, 2 * DIFF_WIDTH, 3 * DIFF_WIDTH,
              3 * DIFF_WIDTH + GQA_WIDTH, 3 * DIFF_WIDTH + GQA_WIDTH + GQA_KV_WIDTH)
PEER_HEADS = 8
PEER_N_KEYS = 128
PEER_N_EXPERTS = PEER_N_KEYS * PEER_N_KEYS
PEER_QUERY_DIM = 256
PEER_HALF = PEER_QUERY_DIM // 2
PEER_TOPK = 16
PEER_CHUNK = 128

kernel_name = "hymba_diffattn_gqa_peer_dit"


def rmsnorm(x, g):
    xf = x.astype(jnp.float32)
    y = xf * lax.rsqrt(jnp.mean(xf * xf, axis=-1, keepdims=True) + NORM_EPS)
    return (y * g.astype(jnp.float32)).astype(x.dtype)


def modulate(h, shift, scale):
    return h * (1 + scale) + shift


def _rope_axis(x, pos):
    d = x.shape[-1]
    half = d // 2
    freqs = ROPE_THETA ** (-jnp.arange(half, dtype=jnp.float32) / half)
    ang = pos.astype(jnp.float32)[:, None] * freqs[None, :]
    shape = (1, x.shape[1]) + (1,) * (x.ndim - 3) + (half,)
    cos = jnp.cos(ang).reshape(shape).astype(x.dtype)
    sin = jnp.sin(ang).reshape(shape).astype(x.dtype)
    x1, x2 = x[..., :half], x[..., half:]
    return jnp.concatenate([x1 * cos - x2 * sin, x1 * sin + x2 * cos], axis=-1)


def rope2d(x):
    n = x.shape[1]
    rows = n // GRID_W
    row_pos = jnp.broadcast_to(jnp.arange(rows, dtype=jnp.int32)[:, None], (rows, GRID_W)).reshape(-1)
    col_pos = jnp.broadcast_to(jnp.arange(GRID_W, dtype=jnp.int32)[None, :], (rows, GRID_W)).reshape(-1)
    d = x.shape[-1] // 2
    return jnp.concatenate([_rope_axis(x[..., :d], row_pos), _rope_axis(x[..., d:], col_pos)], axis=-1)


def attend(q, k, v):
    B, S, Hk, G, d = q.shape
    nb = S // Q_BLOCK
    qb = q.reshape(B, nb, Q_BLOCK, Hk, G, d).swapaxes(0, 1)
    scale = d ** -0.5

    def one(qblk):
        s = jnp.einsum('bqhgd,bkhd->bhgqk', qblk, k, preferred_element_type=jnp.float32) * scale
        p = jax.nn.softmax(s, axis=-1).astype(v.dtype)
        return jnp.einsum('bhgqk,bkhe->bqhge', p, v)

    o = lax.map(one, qb)
    return o.swapaxes(0, 1).reshape(B, S, Hk, G, v.shape[-1])


def _mixer_heads(p):
    B, S, _ = p.shape
    dq, dk, dv, gq, gk, gv = jnp.split(p, _IN_SPLITS, axis=-1)
    return (dq.reshape(B, S, DIFF_HEADS, 2, HEAD_DIM),
            dk.reshape(B, S, DIFF_HEADS, 2, HEAD_DIM),
            dv.reshape(B, S, DIFF_HEADS, DIFF_V_DIM),
            gq.reshape(B, S, GQA_Q_HEADS, HEAD_DIM),
            gk.reshape(B, S, GQA_KV_HEADS, HEAD_DIM),
            gv.reshape(B, S, GQA_KV_HEADS, HEAD_DIM))


def _mix_queries(dq, gq, dk, dv, gk, gv, lam, lambda_init, subln_g, w_out):
    B, S = dq.shape[:2]
    o1 = attend(dq[:, :, :, 0, None, :], dk[:, :, :, 0, :], dv)
    o2 = attend(dq[:, :, :, 1, None, :], dk[:, :, :, 1, :], dv)
    od = (o1 - lam.astype(o1.dtype) * o2)[:, :, :, 0, :]
    od = rmsnorm(od, subln_g) * (1.0 - lambda_init)
    og = attend(gq.reshape(B, S, GQA_KV_HEADS, GQA_GROUP, HEAD_DIM), gk, gv)
    o = jnp.concatenate([od.reshape(B, S, DIFF_WIDTH), og.reshape(B, S, GQA_WIDTH)], axis=-1)
    return o @ w_out


def peer(h, w_q, subkeys, u, v):
    B, S, D = h.shape
    xs = h.reshape(-1, PEER_CHUNK, D)

    def one(xc):
        C = xc.shape[0]
        q = (xc @ w_q).reshape(C, PEER_HEADS, 2, PEER_HALF)
        s = jnp.einsum('chpd,hpnd->chpn', q, subkeys, preferred_element_type=jnp.float32)
        sv, si = lax.top_k(s, PEER_TOPK)
        cand = sv[:, :, 0, :, None] + sv[:, :, 1, None, :]
        cand_id = si[:, :, 0, :, None] * PEER_N_KEYS + si[:, :, 1, None, :]
        best, pos = lax.top_k(cand.reshape(C, PEER_HEADS, PEER_TOPK * PEER_TOPK), PEER_TOPK)
        ids = jnp.take_along_axis(cand_id.reshape(C, PEER_HEADS, PEER_TOPK * PEER_TOPK), pos, axis=-1)
        g = jax.nn.softmax(best, axis=-1)
        ue = jnp.take(u, ids, axis=0)
        ve = jnp.take(v, ids, axis=0)
        a = jax.nn.gelu(jnp.einsum('cd,chkd->chk', xc, ue), approximate=False)
        return jnp.einsum('chk,chkd->cd', (g * a).astype(xc.dtype), ve)

    return lax.map(one, xs).reshape(B, S, D)


def setup_inputs(seed: int = 0) -> dict:
    key = jax.random.key(seed)
    ks = jax.random.split(key, 24)
    D = D_MODEL
    nrm = lambda k, shape, s: jax.random.normal(k, shape, jnp.float32) * s
    return {
        "x": nrm(ks[0], (BATCH, SEQ, D), 1.0),
        "c": nrm(ks[1], (BATCH, D), 1.0),
        "ctx": nrm(ks[2], (BATCH, CTX_LEN, D), 1.0),
        "c_ctx": nrm(ks[3], (D,), 1.0),
        "w_mod": nrm(ks[4], (DEPTH, D, 6 * D), 0.5 * D ** -0.5),
        "b_mod": nrm(ks[5], (DEPTH, 6 * D), 0.01),
        "norm1_g": 1.0 + nrm(ks[6], (DEPTH, D), 0.02),
        "norm2_g": 1.0 + nrm(ks[7], (DEPTH, D), 0.02),
        "w_in": nrm(ks[8], (DEPTH, D, IN_COLS), D ** -0.5),
        "w_out": nrm(ks[9], (DEPTH, MIX_WIDTH, D), MIX_WIDTH ** -0.5),
        "diff_lq1": nrm(ks[10], (DEPTH, HEAD_DIM), 0.1),
        "diff_lk1": nrm(ks[11], (DEPTH, HEAD_DIM), 0.1),
        "diff_lq2": nrm(ks[12], (DEPTH, HEAD_DIM), 0.1),
        "diff_lk2": nrm(ks[13], (DEPTH, HEAD_DIM), 0.1),
        "diff_subln_g": 1.0 + nrm(ks[14], (DEPTH, DIFF_V_DIM), 0.02),
        "gqa_q_norm_g": 1.0 + nrm(ks[15], (DEPTH, HEAD_DIM), 0.02),
        "gqa_k_norm_g": 1.0 + nrm(ks[16], (DEPTH, HEAD_DIM), 0.02),
        "peer_wq": nrm(ks[17], (DEPTH, D, PEER_HEADS * PEER_QUERY_DIM), D ** -0.5),
        "peer_subkeys": nrm(ks[18], (DEPTH, PEER_HEADS, 2, PEER_N_KEYS, PEER_HALF), PEER_HALF ** -0.5),
        "peer_u": nrm(ks[19], (DEPTH, PEER_N_EXPERTS, D), D ** -0.5),
        "peer_v": nrm(ks[20], (DEPTH, PEER_N_EXPERTS, D), (PEER_HEADS * PEER_TOPK) ** -0.5),
        "final_norm_g": 1.0 + nrm(ks[21], (D,), 0.02),
    }


def reference(x, c, ctx, c_ctx, w_mod, b_mod, norm1_g, norm2_g, w_in, w_out,
              diff_lq1, diff_lk1, diff_lq2, diff_lk2, diff_subln_g,
              gqa_q_norm_g, gqa_k_norm_g, peer_wq, peer_subkeys, peer_u, peer_v,
              final_norm_g):
    for i in range(DEPTH):
        last = i == DEPTH - 1
        lambda_init = 0.8 - 0.6 * math.exp(-0.3 * i)
        mod_x = (jax.nn.silu(c) @ w_mod[i] + b_mod[i])[:, None, :]
        mod_c = jax.nn.silu(c_ctx) @ w_mod[i] + b_mod[i]
        sh1x, sc1x, g1x, sh2x, sc2x, g2x = jnp.split(mod_x, 6, axis=-1)
        sh1c, sc1c, g1c, sh2c, sc2c, g2c = jnp.split(mod_c, 6, axis=-1)

        hx = modulate(rmsnorm(x, norm1_g[i]), sh1x, sc1x)
        hc = modulate(rmsnorm(ctx, norm1_g[i]), sh1c, sc1c)
        c_dq, c_dk, c_dv, c_gq, c_gk, c_gv = _mixer_heads(hc @ w_in[i])
        x_dq, x_dk, x_dv, x_gq, x_gk, x_gv = _mixer_heads(hx @ w_in[i])
        c_gq = rmsnorm(c_gq, gqa_q_norm_g[i])
        c_gk = rmsnorm(c_gk, gqa_k_norm_g[i])
        x_gq = rope2d(rmsnorm(x_gq, gqa_q_norm_g[i]))
        x_gk = rope2d(rmsnorm(x_gk, gqa_k_norm_g[i]))
        x_dq = rope2d(x_dq)
        x_dk = rope2d(x_dk)
        lam = (jnp.exp(jnp.sum(diff_lq1[i].astype(jnp.float32) * diff_lk1[i].astype(jnp.float32)))
               - jnp.exp(jnp.sum(diff_lq2[i].astype(jnp.float32) * diff_lk2[i].astype(jnp.float32)))
               + lambda_init)
        out_x = _mix_queries(x_dq, x_gq,
                             jnp.concatenate([c_dk, x_dk], axis=1),
                             jnp.concatenate([c_dv, x_dv], axis=1),
                             jnp.concatenate([c_gk, x_gk], axis=1),
                             jnp.concatenate([c_gv, x_gv], axis=1),
                             lam, lambda_init, diff_subln_g[i], w_out[i])
        if not last:
            out_c = _mix_queries(c_dq, c_gq, c_dk, c_dv, c_gk, c_gv,
                                 lam, lambda_init, diff_subln_g[i], w_out[i])
            ctx = ctx + g1c * out_c
        x = x + g1x * out_x

        hx2 = modulate(rmsnorm(x, norm2_g[i]), sh2x, sc2x)
        x = x + g2x * peer(hx2, peer_wq[i], peer_subkeys[i], peer_u[i], peer_v[i])
        if not last:
            hc2 = modulate(rmsnorm(ctx, norm2_g[i]), sh2c, sc2c)
            ctx = ctx + g2c * peer(hc2, peer_wq[i], peer_subkeys[i], peer_u[i], peer_v[i])
    return rmsnorm(x, final_norm_g)
```

```python
import functools
import math

import jax
import jax.numpy as jnp
from jax import lax
from jax.experimental import pallas as pl
from jax.experimental.pallas import tpu as pltpu

F32 = jnp.float32
BF16 = jnp.bfloat16
I32 = jnp.int32

HEAD_DIM = 64
GRID_W = 64
ROPE_THETA = 10000.0
NORM_EPS = 1e-6
DIFF_HEADS = 4
DIFF_V_DIM = 2 * HEAD_DIM
DIFF_WIDTH = DIFF_HEADS * DIFF_V_DIM
GQA_Q_HEADS = 8
GQA_KV_HEADS = 2
GQA_WIDTH = GQA_Q_HEADS * HEAD_DIM
GQA_KV_WIDTH = GQA_KV_HEADS * HEAD_DIM
PEER_HEADS = 8
PEER_N_KEYS = 128
PEER_HALF = 128
PEER_TOPK = 16
LAMBDA_INIT = 0.8 - 0.6 * math.exp(-0.3 * 0)
LANES = 128
VMEM_LIMIT = 56 * 1024 * 1024

IN_TILE = 512
Q_TILE = 256
OUT_TILE = 256
SEL_TILE = 256
PEER_TILE = 16


def _cparams(sem):
    return pltpu.CompilerParams(dimension_semantics=sem, vmem_limit_bytes=VMEM_LIMIT)


def _split_bf16(a):
    hi = a.astype(BF16)
    lo = (a - hi.astype(F32)).astype(BF16)
    return hi, lo


def _mod_kernel(c_ref, w_ref, b_ref, o_ref):
    s = jax.nn.silu(c_ref[...])
    s_hi, s_lo = _split_bf16(s)
    w_hi, w_lo = _split_bf16(w_ref[...])
    acc = jnp.dot(s_hi, w_hi, preferred_element_type=F32)
    acc += jnp.dot(s_hi, w_lo, preferred_element_type=F32)
    acc += jnp.dot(s_lo, w_hi, preferred_element_type=F32)
    o_ref[...] = acc + b_ref[...]


def _mod_call(cc, w_mod, b_mod):
    rows, d = cc.shape
    n = w_mod.shape[1]
    tn = n // 4
    return pl.pallas_call(
        _mod_kernel,
        grid=(n // tn,),
        in_specs=[pl.BlockSpec((rows, d), lambda j: (0, 0)),
                  pl.BlockSpec((d, tn), lambda j: (0, j)),
                  pl.BlockSpec((1, tn), lambda j: (0, j))],
        out_specs=pl.BlockSpec((rows, tn), lambda j: (0, j)),
        out_shape=jax.ShapeDtypeStruct((rows, n), F32),
        compiler_params=_cparams(("arbitrary",)),
        name="mod",
    )(cc, w_mod, b_mod)


def _rope_tile(x, cos, sa, sb):
    return x * cos + pltpu.roll(x, LANES - 16, 1) * sa + pltpu.roll(x, 16, 1) * sb


def _head_rms(x, bd, g):
    sq = x * x
    hi, lo = _split_bf16(sq)
    ssum = jnp.dot(hi, bd, preferred_element_type=F32) + jnp.dot(lo, bd, preferred_element_type=F32)
    return x * lax.rsqrt(ssum * (1.0 / HEAD_DIM) + NORM_EPS) * g


def _inproj_kernel(x_ref, mod_ref, g1_ref, w_ref, cos_ref, sa_ref, sb_ref, bd_ref, gqg_ref, gkg_ref,
                   *out_refs, with_q, with_rope):
    x = x_ref[0]
    ms = jnp.mean(x * x, axis=-1, keepdims=True)
    y = x * lax.rsqrt(ms + NORM_EPS) * g1_ref[...]
    h = (y * (1.0 + mod_ref[0, 1:2, :]) + mod_ref[0, 0:1, :]).astype(BF16)
    p = jnp.dot(h, w_ref[...], preferred_element_type=F32)

    if with_rope:
        cos, sa, sb = cos_ref[...], sa_ref[...], sb_ref[...]
        rope = lambda t: _rope_tile(t, cos, sa, sb)
    else:
        rope = lambda t: t
    bd = bd_ref[...]
    scale = HEAD_DIM ** -0.5

    if with_q:
        dq_ref, gq_ref, dk_ref, gk_ref, dvt_ref, gvt_ref = out_refs
        off = 0
        for t in range(DIFF_WIDTH // LANES):
            sl = slice(t * LANES, (t + 1) * LANES)
            dq_ref[0, :, sl] = (rope(p[:, off + t * LANES: off + (t + 1) * LANES]) * scale).astype(BF16)
        off += DIFF_WIDTH
    else:
        dk_ref, gk_ref, dvt_ref, gvt_ref = out_refs
        off = 0
    for t in range(DIFF_WIDTH // LANES):
        sl = slice(t * LANES, (t + 1) * LANES)
        dk_ref[0, :, sl] = rope(p[:, off + t * LANES: off + (t + 1) * LANES]).astype(BF16)
    off += DIFF_WIDTH
    dvt_ref[0, 0] = p[:, off: off + DIFF_WIDTH].T.astype(BF16)
    off += DIFF_WIDTH
    if with_q:
        gqn = _head_rms(p[:, off: off + GQA_WIDTH], bd, gqg_ref[...])
        for t in range(GQA_WIDTH // LANES):
            sl = slice(t * LANES, (t + 1) * LANES)
            gq_ref[0, :, sl] = (rope(gqn[:, sl]) * scale).astype(BF16)
        off += GQA_WIDTH
    gkn = _head_rms(p[:, off: off + GQA_KV_WIDTH], bd[:GQA_KV_WIDTH, :GQA_KV_WIDTH], gkg_ref[...])
    gk_ref[0] = rope(gkn).astype(BF16)
    off += GQA_KV_WIDTH
    gvt_ref[0, 0] = p[:, off: off + GQA_KV_WIDTH].T.astype(BF16)


def _inproj_call(x, mod3, mod_row, g1, w, tables, bd, gqg, gkg, *, with_q, tile):
    b, s, d = x.shape
    n = w.shape[1]
    nt = s // tile
    cos, sa, sb = tables
    kv_shapes = [jax.ShapeDtypeStruct((b, s, DIFF_WIDTH), BF16),
                 jax.ShapeDtypeStruct((b, s, GQA_KV_WIDTH), BF16),
                 jax.ShapeDtypeStruct((b, nt, DIFF_WIDTH, tile), BF16),
                 jax.ShapeDtypeStruct((b, nt, GQA_KV_WIDTH, tile), BF16)]
    kv_specs = [pl.BlockSpec((1, tile, DIFF_WIDTH), lambda bi, i: (bi, i, 0)),
                pl.BlockSpec((1, tile, GQA_KV_WIDTH), lambda bi, i: (bi, i, 0)),
                pl.BlockSpec((1, 1, DIFF_WIDTH, tile), lambda bi, i: (bi, i, 0, 0)),
                pl.BlockSpec((1, 1, GQA_KV_WIDTH, tile), lambda bi, i: (bi, i, 0, 0))]
    if with_q:
        out_shapes = [jax.ShapeDtypeStruct((b, s, DIFF_WIDTH), BF16),
                      jax.ShapeDtypeStruct((b, s, GQA_WIDTH), BF16)] + kv_shapes
        out_specs = [pl.BlockSpec((1, tile, DIFF_WIDTH), lambda bi, i: (bi, i, 0)),
                     pl.BlockSpec((1, tile, GQA_WIDTH), lambda bi, i: (bi, i, 0))] + kv_specs
    else:
        out_shapes, out_specs = kv_shapes, kv_specs
    const = lambda bi, i: (0, 0)
    return pl.pallas_call(
        functools.partial(_inproj_kernel, with_q=with_q, with_rope=with_q),
        grid=(b, nt),
        in_specs=[pl.BlockSpec((1, tile, d), lambda bi, i: (bi, i, 0)),
                  pl.BlockSpec((1, 6, d), mod_row),
                  pl.BlockSpec((1, d), const),
                  pl.BlockSpec((d, n), const),
                  pl.BlockSpec((tile, LANES), lambda bi, i: (i, 0)),
                  pl.BlockSpec((tile, LANES), lambda bi, i: (i, 0)),
                  pl.BlockSpec((tile, LANES), lambda bi, i: (i, 0)),
                  pl.BlockSpec(bd.shape, const),
                  pl.BlockSpec(gqg.shape, const),
                  pl.BlockSpec(gkg.shape, const)],
        out_specs=out_specs,
        out_shape=out_shapes,
        compiler_params=_cparams(("parallel", "parallel")),
        name="inproj_x" if with_q else "inproj_ctx",
    )(x, mod3, g1, w, cos, sa, sb, bd, gqg, gkg)


def _attn_step(k, q, vt, carry):
    m, l, acc = carry
    s = lax.dot_general(k, q, (((1,), (1,)), ((), ())), preferred_element_type=F32)
    m_new = jnp.maximum(m, jnp.max(s, axis=0, keepdims=True))
    alpha = jnp.exp(m - m_new)
    p = jnp.exp(s - m_new)
    l = alpha * l + jnp.sum(p, axis=0, keepdims=True)
    acc = alpha * acc + jnp.dot(vt, p.astype(BF16), preferred_element_type=F32)
    return m_new, l, acc


def _attn_kernel(dq_ref, gq_ref, dkx_ref, gkx_ref, dvx_ref, gvx_ref, dkc_ref, gkc_ref, dvc_ref, gvc_ref,
                 lamv_ref, subg_ref, o_ref):
    tq = dq_ref.shape[1]
    n_chunks, chunk = dvx_ref.shape[1], dvx_ref.shape[3]
    lv = lamv_ref[...]
    lam = (jnp.exp(jnp.sum(lv[0:1] * lv[1:2], axis=-1, keepdims=True))
           - jnp.exp(jnp.sum(lv[2:3] * lv[3:4], axis=-1, keepdims=True)) + LAMBDA_INIT)
    low = lax.broadcasted_iota(I32, (tq, LANES), 1) < HEAD_DIM

    def unit(q, kx_ref, kc_ref, col, vx_ref, vc_ref, r0, dv):
        carry = (jnp.full((1, tq), -jnp.inf, F32), jnp.zeros((1, tq), F32), jnp.zeros((dv, tq), F32))
        carry = _attn_step(kc_ref[0, :, col:col + LANES], q, vc_ref[0, 0, r0:r0 + dv, :], carry)

        def body(c, carry):
            off = pl.multiple_of(c * chunk, chunk)
            return _attn_step(kx_ref[0, pl.ds(off, chunk), col:col + LANES], q,
                              vx_ref[0, c, r0:r0 + dv, :], carry)

        _, l, acc = lax.fori_loop(0, n_chunks, body, carry)
        return acc / l

    zero = jnp.zeros((tq, LANES), BF16)
    for h in range(DIFF_HEADS):
        col = h * LANES
        qh = dq_ref[0, :, col:col + LANES]
        o1 = unit(jnp.where(low, qh, zero), dkx_ref, dkc_ref, col, dvx_ref, dvc_ref, h * DIFF_V_DIM, DIFF_V_DIM)
        o2 = unit(jnp.where(low, zero, qh), dkx_ref, dkc_ref, col, dvx_ref, dvc_ref, h * DIFF_V_DIM, DIFF_V_DIM)
        od = o1 - lam * o2
        ms = jnp.mean(od * od, axis=0, keepdims=True)
        od = od * lax.rsqrt(ms + NORM_EPS) * subg_ref[...] * (1.0 - LAMBDA_INIT)
        o_ref[0, :, col:col + LANES] = od.T.astype(BF16)
    for t in range(GQA_Q_HEADS // 2):
        col = t * LANES
        qt = gq_ref[0, :, col:col + LANES]
        og0 = unit(jnp.where(low, qt, zero), gkx_ref, gkc_ref, 0, gvx_ref, gvc_ref, 0, HEAD_DIM)
        og1 = unit(jnp.where(low, zero, qt), gkx_ref, gkc_ref, 0, gvx_ref, gvc_ref, HEAD_DIM, HEAD_DIM)
        og = jnp.concatenate([og0, og1], axis=0)
        o_ref[0, :, DIFF_WIDTH + col: DIFF_WIDTH + col + LANES] = og.T.astype(BF16)


def _attn_call(dq, gq, dkx, gkx, dvx, gvx, dkc, gkc, dvc, gvc, lamv, subg):
    b, s, _ = dq.shape
    nq = s // Q_TILE
    full3 = lambda a: pl.BlockSpec((1,) + a.shape[1:], lambda bi, i: (bi, 0, 0))
    full4 = lambda a: pl.BlockSpec((1,) + a.shape[1:], lambda bi, i: (bi, 0, 0, 0))
    const = lambda bi, i: (0, 0)
    return pl.pallas_call(
        _attn_kernel,
        grid=(b, nq),
        in_specs=[pl.BlockSpec((1, Q_TILE, DIFF_WIDTH), lambda bi, i: (bi, i, 0)),
                  pl.BlockSpec((1, Q_TILE, GQA_WIDTH), lambda bi, i: (bi, i, 0)),
                  full3(dkx), full3(gkx), full4(dvx), full4(gvx),
                  full3(dkc), full3(gkc), full4(dvc), full4(gvc),
                  pl.BlockSpec(lamv.shape, const), pl.BlockSpec(subg.shape, const)],
        out_specs=pl.BlockSpec((1, Q_TILE, DIFF_WIDTH + GQA_WIDTH), lambda bi, i: (bi, i, 0)),
        out_shape=jax.ShapeDtypeStruct((b, s, DIFF_WIDTH + GQA_WIDTH), BF16),
        compiler_params=_cparams(("parallel", "arbitrary")),
        name="attn",
    )(dq, gq, dkx, gkx, dvx, gvx, dkc, gkc, dvc, gvc, lamv, subg)


def _out_kernel(o_ref, w_ref, x_ref, mod_ref, g2_ref, x1_ref, h2_ref):
    attn = jnp.dot(o_ref[0], w_ref[...], preferred_element_type=F32)
    x1 = x_ref[0] + mod_ref[0, 2:3, :] * attn
    x1_ref[0] = x1
    ms = jnp.mean(x1 * x1, axis=-1, keepdims=True)
    y = x1 * lax.rsqrt(ms + NORM_EPS) * g2_ref[...]
    h2_ref[0] = (y * (1.0 + mod_ref[0, 4:5, :]) + mod_ref[0, 3:4, :]).astype(BF16)


def _out_call(o, w_out, x, mod3, g2):
    b, s, d = x.shape
    const = lambda bi, i: (0, 0)
    tile = lambda w: pl.BlockSpec((1, OUT_TILE, w), lambda bi, i: (bi, i, 0))
    return pl.pallas_call(
        _out_kernel,
        grid=(b, s // OUT_TILE),
        in_specs=[tile(o.shape[2]), pl.BlockSpec(w_out.shape, const), tile(d),
                  pl.BlockSpec((1, 6, d), lambda bi, i: (bi, 0, 0)), pl.BlockSpec((1, d), const)],
        out_specs=[tile(d), tile(d)],
        out_shape=[jax.ShapeDtypeStruct((b, s, d), F32), jax.ShapeDtypeStruct((b, s, d), BF16)],
        compiler_params=_cparams(("parallel", "parallel")),
        name="outproj",
    )(o, w_out, x, mod3, g2)


def _topk_sublanes(s, flat, k):
    t = s.shape[1]
    big = jnp.iinfo(jnp.int32).max
    rowk = lax.broadcasted_iota(I32, (k, t), 0)
    vals = jnp.zeros((k, t), F32)
    labs = jnp.zeros((k, t), I32)
    for r in range(k):
        m = jnp.max(s, axis=0, keepdims=True)
        lab = jnp.min(jnp.where(s == m, flat, big), axis=0, keepdims=True)
        vals = jnp.where(rowk == r, m, vals)
        labs = jnp.where(rowk == r, lab, labs)
        s = jnp.where(flat == lab, -jnp.inf, s)
    return vals, labs


def _sel_kernel(h_ref, wqt_ref, sk_ref, ids_ref, gates_ref, qt_ref, idst_ref):
    ts = h_ref.shape[0]
    k = PEER_TOPK
    qt_ref[...] = lax.dot_general(wqt_ref[...], h_ref[...], (((1,), (1,)), ((), ())),
                                  preferred_element_type=F32).astype(BF16)
    key_iota = lax.broadcasted_iota(I32, (PEER_N_KEYS, ts), 0)
    sub8 = lax.broadcasted_iota(I32, (8, ts), 0)
    flat_rows = [lax.broadcasted_iota(I32, (k, ts), 0)] + [i * k + sub8 for i in range(1, k)]
    cand_flat = jnp.concatenate(flat_rows, axis=0)

    def head(h, carry):
        r0 = pl.multiple_of(h * 2 * PEER_HALF, 2 * PEER_HALF)
        tops = []
        for p in range(2):
            q = qt_ref[pl.ds(r0 + p * PEER_HALF, PEER_HALF), :]
            s = jnp.dot(sk_ref[2 * h + p], q, preferred_element_type=F32)
            tops.append(_topk_sublanes(s, key_iota, k))
        (v1, i1), (v2, i2) = tops
        vals = [v1[0:1] + v2] + [v1[i:i + 1] + v2[:8] for i in range(1, k)]
        eids = [i1[0:1] * PEER_N_KEYS + i2] + [i1[i:i + 1] * PEER_N_KEYS + i2[:8] for i in range(1, k)]
        cand = jnp.concatenate(vals, axis=0)
        cid = jnp.concatenate(eids, axis=0)
        best, pos = _topk_sublanes(cand, cand_flat, k)
        rowk = lax.broadcasted_iota(I32, (k, ts), 0)
        ids = jnp.zeros((k, ts), I32)
        for r in range(k):
            pick = jnp.sum(jnp.where(cand_flat == pos[r:r + 1], cid, 0), axis=0, keepdims=True)
            ids = jnp.where(rowk == r, pick, ids)
        e = jnp.exp(best - best[0:1])
        o0 = pl.multiple_of(h * k, k)
        gates_ref[pl.ds(o0, k), :] = e / jnp.sum(e, axis=0, keepdims=True)
        idst_ref[pl.ds(o0, k), :] = ids
        return carry

    lax.fori_loop(0, PEER_HEADS, head, 0)
    ids_ref[...] = idst_ref[...].T


def _sel_call(h2, wqt, sk):
    t, d = h2.shape
    nsel = PEER_HEADS * PEER_TOPK
    return pl.pallas_call(
        _sel_kernel,
        grid=(t // SEL_TILE,),
        in_specs=[pl.BlockSpec((SEL_TILE, d), lambda i: (i, 0)),
                  pl.BlockSpec(wqt.shape, lambda i: (0, 0)),
                  pl.BlockSpec(sk.shape, lambda i: (0, 0, 0))],
        out_specs=[pl.BlockSpec((SEL_TILE, nsel), lambda i: (i, 0)),
                   pl.BlockSpec((nsel, SEL_TILE), lambda i: (0, i))],
        out_shape=[jax.ShapeDtypeStruct((t, nsel), I32), jax.ShapeDtypeStruct((nsel, t), F32)],
        scratch_shapes=[pltpu.VMEM((wqt.shape[0], SEL_TILE), BF16), pltpu.VMEM((nsel, SEL_TILE), I32)],
        compiler_params=_cparams(("parallel",)),
        name="peer_select",
    )(h2, wqt, sk)


def _peer_kernel(ids_next_ref, ids_first_ref, gates_ref, h_ref, x1_ref, mod_ref, fg_ref, uv_hbm, o_ref,
                 gbuf, sem, mix_ref, *, nsteps):
    tt = h_ref.shape[0]
    nsel = gates_ref.shape[1]
    d = h_ref.shape[1]
    rows = tt * nsel
    i = pl.program_id(0)
    slot = lax.rem(i, 2)

    def row_copy(ids_ref, r, sl):
        idx = ids_ref[r // nsel, r % nsel]
        return pltpu.make_async_copy(uv_hbm.at[pl.ds(idx, 1)], gbuf.at[sl, pl.ds(r, 1)], sem.at[sl])

    def issue(ids_ref, sl):
        def body(r, c):
            row_copy(ids_ref, r, sl).start()
            return c
        lax.fori_loop(0, rows, body, 0)

    @pl.when(i == 0)
    def _():
        issue(ids_first_ref, 0)

    @pl.when(i + 1 < nsteps)
    def _():
        issue(ids_next_ref, 1 - slot)

    pltpu.make_async_copy(uv_hbm.at[pl.ds(0, rows)], gbuf.at[slot], sem.at[slot]).wait()

    hx = h_ref[...].astype(F32)
    lane_t = lax.broadcasted_iota(I32, (nsel, tt), 1)
    a = jnp.zeros((nsel, tt), F32)
    for t in range(tt):
        u = gbuf[slot, t * nsel:(t + 1) * nsel, 0:d]
        prod = u * hx[t:t + 1, :]
        part = prod[:, 0:LANES]
        for c in range(1, d // LANES):
            part = part + prod[:, c * LANES:(c + 1) * LANES]
        a = jnp.where(lane_t == t, jnp.sum(part, axis=1, keepdims=True), a)
    gelu = 0.5 * a * (1.0 + lax.erf(a * (2.0 ** -0.5)))
    w = gates_ref[0] * gelu
    for t in range(tt):
        v = gbuf[slot, t * nsel:(t + 1) * nsel, d:2 * d]
        mix_ref[t:t + 1, :] = jnp.sum(v * w[:, t:t + 1], axis=0, keepdims=True)
    y = x1_ref[...] + mod_ref[0, 5:6, :] * mix_ref[...]
    ms = jnp.mean(y * y, axis=-1, keepdims=True)
    o_ref[...] = y * lax.rsqrt(ms + NORM_EPS) * fg_ref[...]


def _peer_call(ids, gates_r, h2, x1, mod3, fg, uv, seq):
    t, d = h2.shape
    nsel = ids.shape[1]
    tt = PEER_TILE
    nsteps = t // tt
    per_batch = seq // tt
    return pl.pallas_call(
        functools.partial(_peer_kernel, nsteps=nsteps),
        grid=(nsteps,),
        in_specs=[pl.BlockSpec((tt, nsel), lambda i: (jnp.minimum(i + 1, nsteps - 1), 0), memory_space=pltpu.SMEM),
                  pl.BlockSpec((tt, nsel), lambda i: (0, 0), memory_space=pltpu.SMEM),
                  pl.BlockSpec((1, nsel, tt), lambda i: (i, 0, 0)),
                  pl.BlockSpec((tt, d), lambda i: (i, 0)),
                  pl.BlockSpec((tt, d), lambda i: (i, 0)),
                  pl.BlockSpec((1, 6, d), lambda i: (i // per_batch, 0, 0)),
                  pl.BlockSpec((1, d), lambda i: (0, 0)),
                  pl.BlockSpec(memory_space=pl.ANY)],
        out_specs=pl.BlockSpec((tt, d), lambda i: (i, 0)),
        out_shape=jax.ShapeDtypeStruct((t, d), F32),
        scratch_shapes=[pltpu.VMEM((2, tt * nsel, 2 * d), F32),
                        pltpu.SemaphoreType.DMA((2,)),
                        pltpu.VMEM((tt, d), F32)],
        compiler_params=_cparams(("arbitrary",)),
        name="peer_mix",
    )(ids, ids, gates_r, h2, x1, mod3, fg, uv)


def _rope_tables(seq):
    half = HEAD_DIM // 4
    freqs = ROPE_THETA ** (-jnp.arange(half, dtype=F32) / half)
    pos = jnp.arange(seq, dtype=jnp.int32)
    ang_r = (pos // GRID_W).astype(F32)[:, None] * freqs[None, :]
    ang_c = (pos % GRID_W).astype(F32)[:, None] * freqs[None, :]
    z = jnp.zeros_like(ang_r)
    cos = jnp.concatenate([jnp.cos(ang_r)] * 2 + [jnp.cos(ang_c)] * 2, axis=1)
    sa = jnp.concatenate([-jnp.sin(ang_r), z, -jnp.sin(ang_c), z], axis=1)
    sb = jnp.concatenate([z, jnp.sin(ang_r), z, jnp.sin(ang_c)], axis=1)
    rep = LANES // HEAD_DIM
    return tuple(jnp.tile(a, (1, rep)) for a in (cos, sa, sb))


def kernel(x, c, ctx, c_ctx, w_mod, b_mod, norm1_g, norm2_g, w_in, w_out, diff_lq1, diff_lk1, diff_lq2,
           diff_lk2, diff_subln_g, gqa_q_norm_g, gqa_k_norm_g, peer_wq, peer_subkeys, peer_u, peer_v,
           final_norm_g):
    b, s, d = x.shape
    assert w_mod.shape[0] == 1, "depth-1 block"
    assert s % IN_TILE == 0 and s % Q_TILE == 0 and ctx.shape[1] % LANES == 0

    pad = (-(b + 1)) % 8
    cc = jnp.concatenate([c, c_ctx[None, :], jnp.zeros((pad, d), F32)], axis=0)
    mod3 = _mod_call(cc, w_mod[0], b_mod[0][None, :]).reshape(cc.shape[0], 6, d)

    w0 = w_in[0]
    o_dk, o_dv, o_gq, o_gk, o_gv = (DIFF_WIDTH, 2 * DIFF_WIDTH, 3 * DIFF_WIDTH, 3 * DIFF_WIDTH + GQA_WIDTH,
                                    3 * DIFF_WIDTH + GQA_WIDTH + GQA_KV_WIDTH)
    group = GQA_Q_HEADS // GQA_KV_HEADS
    head_order = [j * group + t for t in range(group) for j in range(GQA_KV_HEADS)]
    gq_cols = jnp.asarray([hd * HEAD_DIM + e for hd in head_order for e in range(HEAD_DIM)], dtype=jnp.int32)
    w_gq = w0[:, o_gq:o_gk][:, gq_cols]
    w_x = jnp.concatenate([w0[:, :o_gq], w_gq, w0[:, o_gk:]], axis=1).astype(BF16)
    w_c = jnp.concatenate([w0[:, o_dk:o_gq], w0[:, o_gk:]], axis=1).astype(BF16)
    w_o = jnp.concatenate([w_out[0][:DIFF_WIDTH], w_out[0][DIFF_WIDTH:][gq_cols]], axis=0).astype(BF16)

    seg = jnp.arange(GQA_WIDTH) // HEAD_DIM
    bd = (seg[:, None] == seg[None, :]).astype(BF16)
    gqg = jnp.tile(gqa_q_norm_g[0], GQA_Q_HEADS)[None, :]
    gkg = jnp.tile(gqa_k_norm_g[0], GQA_KV_HEADS)[None, :]
    g1 = norm1_g[0][None, :]
    tables = _rope_tables(s)

    dq, gq, dkx, gkx, dvx, gvx = _inproj_call(x, mod3, lambda bi, i: (bi, 0, 0), g1, w_x, tables, bd, gqg, gkg,
                                              with_q=True, tile=IN_TILE)
    ctx_tile = ctx.shape[1]
    ctx_tables = tuple(a[:ctx_tile] for a in tables)
    dkc, gkc, dvc, gvc = _inproj_call(ctx, mod3, lambda bi, i: (b, 0, 0), g1, w_c, ctx_tables, bd, gqg, gkg,
                                      with_q=False, tile=ctx_tile)

    lamv = jnp.stack([diff_lq1[0], diff_lk1[0], diff_lq2[0], diff_lk2[0]], axis=0).astype(F32)
    subg = diff_subln_g[0][:, None]
    o = _attn_call(dq, gq, dkx, gkx, dvx, gvx, dkc, gkc, dvc, gvc, lamv, subg)

    x1, h2 = _out_call(o, w_o, x, mod3, norm2_g[0][None, :])

    t = b * s
    h2f = h2.reshape(t, d)
    wqt = peer_wq[0].T.astype(BF16)
    sk = peer_subkeys[0].reshape(PEER_HEADS * 2, PEER_N_KEYS, PEER_HALF).astype(BF16)
    ids, gates_t = _sel_call(h2f, wqt, sk)

    nsel = ids.shape[1]
    gates_r = gates_t.reshape(nsel, t // PEER_TILE, PEER_TILE).transpose(1, 0, 2)
    uv = jnp.concatenate([peer_u[0], peer_v[0]], axis=1)
    out = _peer_call(ids, gates_r, h2f, x1.reshape(t, d), mod3, final_norm_g[None, :], uv, s)
    return out.reshape(b, s, d)
```

```python
import functools
import math

import jax
import jax.numpy as jnp
from jax import lax
from jax.experimental import pallas as pl
from jax.experimental.pallas import tpu as pltpu

F32 = jnp.float32
BF16 = jnp.bfloat16
I32 = jnp.int32

HEAD_DIM = 64
GRID_W = 64
ROPE_THETA = 10000.0
NORM_EPS = 1e-6
DIFF_HEADS = 4
DIFF_V_DIM = 2 * HEAD_DIM
DIFF_WIDTH = DIFF_HEADS * DIFF_V_DIM
GQA_Q_HEADS = 8
GQA_KV_HEADS = 2
GQA_WIDTH = GQA_Q_HEADS * HEAD_DIM
GQA_KV_WIDTH = GQA_KV_HEADS * HEAD_DIM
PEER_HEADS = 8
PEER_N_KEYS = 128
PEER_HALF = 128
PEER_TOPK = 16
LAMBDA_INIT = 0.8 - 0.6 * math.exp(-0.3 * 0)
LANES = 128
VMEM_LIMIT = 56 * 1024 * 1024

IN_TILE = 512
Q_TILE = 256
OUT_TILE = 256
SEL_TILE = 256
PEER_TILE = 16
ATTN_GROUP = 8


def _cparams(sem):
    return pltpu.CompilerParams(dimension_semantics=sem, vmem_limit_bytes=VMEM_LIMIT)


def _split_bf16(a):
    hi = a.astype(BF16)
    lo = (a - hi.astype(F32)).astype(BF16)
    return hi, lo


def _mod_kernel(c_ref, w_ref, b_ref, o_ref):
    s = jax.nn.silu(c_ref[...])
    s_hi, s_lo = _split_bf16(s)
    w_hi, w_lo = _split_bf16(w_ref[...])
    acc = jnp.dot(s_hi, w_hi, preferred_element_type=F32)
    acc += jnp.dot(s_hi, w_lo, preferred_element_type=F32)
    acc += jnp.dot(s_lo, w_hi, preferred_element_type=F32)
    o_ref[...] = acc + b_ref[...]


def _mod_call(cc, w_mod, b_mod):
    rows, d = cc.shape
    n = w_mod.shape[1]
    tn = n // 4
    return pl.pallas_call(
        _mod_kernel,
        grid=(n // tn,),
        in_specs=[pl.BlockSpec((rows, d), lambda j: (0, 0)),
                  pl.BlockSpec((d, tn), lambda j: (0, j)),
                  pl.BlockSpec((1, tn), lambda j: (0, j))],
        out_specs=pl.BlockSpec((rows, tn), lambda j: (0, j)),
        out_shape=jax.ShapeDtypeStruct((rows, n), F32),
        compiler_params=_cparams(("arbitrary",)),
        name="mod",
    )(cc, w_mod, b_mod)


def _rope_tile(x, cos, sa, sb):
    return x * cos + pltpu.roll(x, LANES - 16, 1) * sa + pltpu.roll(x, 16, 1) * sb


def _head_rms(x, bd, g):
    sq = x * x
    hi, lo = _split_bf16(sq)
    ssum = jnp.dot(hi, bd, preferred_element_type=F32) + jnp.dot(lo, bd, preferred_element_type=F32)
    return x * lax.rsqrt(ssum * (1.0 / HEAD_DIM) + NORM_EPS) * g


def _inproj_kernel(x_ref, mod_ref, g1_ref, w_ref, cos_ref, sa_ref, sb_ref, bd_ref, gqg_ref, gkg_ref,
                   *out_refs, with_q, with_rope):
    x = x_ref[0]
    ms = jnp.mean(x * x, axis=-1, keepdims=True)
    y = x * lax.rsqrt(ms + NORM_EPS) * g1_ref[...]
    h = (y * (1.0 + mod_ref[0, 1:2, :]) + mod_ref[0, 0:1, :]).astype(BF16)
    p = jnp.dot(h, w_ref[...], preferred_element_type=F32)

    if with_rope:
        cos, sa, sb = cos_ref[...], sa_ref[...], sb_ref[...]
        rope = lambda t: _rope_tile(t, cos, sa, sb)
    else:
        rope = lambda t: t
    bd = bd_ref[...]
    scale = HEAD_DIM ** -0.5

    if with_q:
        dq_ref, gq_ref, dk_ref, gk_ref, dvt_ref, gvt_ref = out_refs
        off = 0
        for t in range(DIFF_WIDTH // LANES):
            sl = slice(t * LANES, (t + 1) * LANES)
            dq_ref[0, :, sl] = (rope(p[:, off + t * LANES: off + (t + 1) * LANES]) * scale).astype(BF16)
        off += DIFF_WIDTH
    else:
        dk_ref, gk_ref, dvt_ref, gvt_ref = out_refs
        off = 0
    for t in range(DIFF_WIDTH // LANES):
        sl = slice(t * LANES, (t + 1) * LANES)
        dk_ref[0, :, sl] = rope(p[:, off + t * LANES: off + (t + 1) * LANES]).astype(BF16)
    off += DIFF_WIDTH
    dvt_ref[0, 0] = p[:, off: off + DIFF_WIDTH].T.astype(BF16)
    off += DIFF_WIDTH
    if with_q:
        gqn = _head_rms(p[:, off: off + GQA_WIDTH], bd, gqg_ref[...])
        for t in range(GQA_WIDTH // LANES):
            sl = slice(t * LANES, (t + 1) * LANES)
            gq_ref[0, :, sl] = (rope(gqn[:, sl]) * scale).astype(BF16)
        off += GQA_WIDTH
    gkn = _head_rms(p[:, off: off + GQA_KV_WIDTH], bd[:GQA_KV_WIDTH, :GQA_KV_WIDTH], gkg_ref[...])
    gk_ref[0] = rope(gkn).astype(BF16)
    off += GQA_KV_WIDTH
    gvt_ref[0, 0] = p[:, off: off + GQA_KV_WIDTH].T.astype(BF16)


def _inproj_call(x, mod3, mod_row, g1, w, tables, bd, gqg, gkg, *, with_q, tile):
    b, s, d = x.shape
    n = w.shape[1]
    nt = s // tile
    cos, sa, sb = tables
    kv_shapes = [jax.ShapeDtypeStruct((b, s, DIFF_WIDTH), BF16),
                 jax.ShapeDtypeStruct((b, s, GQA_KV_WIDTH), BF16),
                 jax.ShapeDtypeStruct((b, nt, DIFF_WIDTH, tile), BF16),
                 jax.ShapeDtypeStruct((b, nt, GQA_KV_WIDTH, tile), BF16)]
    kv_specs = [pl.BlockSpec((1, tile, DIFF_WIDTH), lambda bi, i: (bi, i, 0)),
                pl.BlockSpec((1, tile, GQA_KV_WIDTH), lambda bi, i: (bi, i, 0)),
                pl.BlockSpec((1, 1, DIFF_WIDTH, tile), lambda bi, i: (bi, i, 0, 0)),
                pl.BlockSpec((1, 1, GQA_KV_WIDTH, tile), lambda bi, i: (bi, i, 0, 0))]
    if with_q:
        out_shapes = [jax.ShapeDtypeStruct((b, s, DIFF_WIDTH), BF16),
                      jax.ShapeDtypeStruct((b, s, GQA_WIDTH), BF16)] + kv_shapes
        out_specs = [pl.BlockSpec((1, tile, DIFF_WIDTH), lambda bi, i: (bi, i, 0)),
                     pl.BlockSpec((1, tile, GQA_WIDTH), lambda bi, i: (bi, i, 0))] + kv_specs
    else:
        out_shapes, out_specs = kv_shapes, kv_specs
    const = lambda bi, i: (0, 0)
    return pl.pallas_call(
        functools.partial(_inproj_kernel, with_q=with_q, with_rope=with_q),
        grid=(b, nt),
        in_specs=[pl.BlockSpec((1, tile, d), lambda bi, i: (bi, i, 0)),
                  pl.BlockSpec((1, 6, d), mod_row),
                  pl.BlockSpec((1, d), const),
                  pl.BlockSpec((d, n), const),
                  pl.BlockSpec((tile, LANES), lambda bi, i: (i, 0)),
                  pl.BlockSpec((tile, LANES), lambda bi, i: (i, 0)),
                  pl.BlockSpec((tile, LANES), lambda bi, i: (i, 0)),
                  pl.BlockSpec(bd.shape, const),
                  pl.BlockSpec(gqg.shape, const),
                  pl.BlockSpec(gkg.shape, const)],
        out_specs=out_specs,
        out_shape=out_shapes,
        compiler_params=_cparams(("parallel", "parallel")),
        name="inproj_x" if with_q else "inproj_ctx",
    )(x, mod3, g1, w, cos, sa, sb, bd, gqg, gkg)


def _attn_steps(ks, qs, vts, carries):
    ss = [lax.dot_general(k, q, (((1,), (1,)), ((), ())), preferred_element_type=F32) for k, q in zip(ks, qs)]
    stats = []
    for s, (m, l, _) in zip(ss, carries):
        m_new = jnp.maximum(m, jnp.max(s, axis=0, keepdims=True))
        alpha = jnp.exp(m - m_new)
        p = jnp.exp(s - m_new)
        stats.append((m_new, alpha, alpha * l + jnp.sum(p, axis=0, keepdims=True), p.astype(BF16)))
    return tuple((m_new, l, alpha * acc + jnp.dot(vt, p, preferred_element_type=F32))
                 for (m_new, alpha, l, p), vt, (_, _, acc) in zip(stats, vts, carries))


def _attn_kernel(dq_ref, gq_ref, dkx_ref, gkx_ref, dvx_ref, gvx_ref, dkc_ref, gkc_ref, dvc_ref, gvc_ref,
                 lamv_ref, subg_ref, o_ref):
    tq = dq_ref.shape[1]
    n_chunks, chunk = dvx_ref.shape[1], dvx_ref.shape[3]
    lv = lamv_ref[...]
    lam = (jnp.exp(jnp.sum(lv[0:1] * lv[1:2], axis=-1, keepdims=True))
           - jnp.exp(jnp.sum(lv[2:3] * lv[3:4], axis=-1, keepdims=True)) + LAMBDA_INIT)
    low = lax.broadcasted_iota(I32, (tq, LANES), 1) < HEAD_DIM

    def run_units(units):
        def go(kx_ref, kc_ref, vx_ref, vc_ref, dv):
            qs = [q for q, _, _ in units]
            init = tuple((jnp.full((1, tq), -jnp.inf, F32), jnp.zeros((1, tq), F32), jnp.zeros((dv, tq), F32))
                         for _ in units)
            carries = _attn_steps([kc_ref[0, :, col:col + LANES] for _, col, _ in units], qs,
                                  [vc_ref[0, 0, r0:r0 + dv, :] for _, _, r0 in units], init)

            def body(c, carries):
                off = pl.multiple_of(c * chunk, chunk)
                return _attn_steps([kx_ref[0, pl.ds(off, chunk), col:col + LANES] for _, col, _ in units], qs,
                                   [vx_ref[0, c, r0:r0 + dv, :] for _, _, r0 in units], carries)

            return [acc / l for _, l, acc in lax.fori_loop(0, n_chunks, body, carries)]
        return go

    zero = jnp.zeros((tq, LANES), BF16)
    for h0 in range(0, DIFF_HEADS, ATTN_GROUP // 2):
        heads = range(h0, h0 + ATTN_GROUP // 2)
        units = []
        for h in heads:
            qh = dq_ref[0, :, h * LANES:(h + 1) * LANES]
            units += [(jnp.where(low, qh, zero), h * LANES, h * DIFF_V_DIM),
                      (jnp.where(low, zero, qh), h * LANES, h * DIFF_V_DIM)]
        outs = run_units(units)(dkx_ref, dkc_ref, dvx_ref, dvc_ref, DIFF_V_DIM)
        for n, h in enumerate(heads):
            od = outs[2 * n] - lam * outs[2 * n + 1]
            ms = jnp.mean(od * od, axis=0, keepdims=True)
            od = od * lax.rsqrt(ms + NORM_EPS) * subg_ref[...] * (1.0 - LAMBDA_INIT)
            o_ref[0, :, h * LANES:(h + 1) * LANES] = od.T.astype(BF16)
    for t0 in range(0, GQA_Q_HEADS // 2, ATTN_GROUP // 2):
        tiles = range(t0, t0 + ATTN_GROUP // 2)
        units = []
        for t in tiles:
            qt = gq_ref[0, :, t * LANES:(t + 1) * LANES]
            units += [(jnp.where(low, qt, zero), 0, 0), (jnp.where(low, zero, qt), 0, HEAD_DIM)]
        outs = run_units(units)(gkx_ref, gkc_ref, gvx_ref, gvc_ref, HEAD_DIM)
        for n, t in enumerate(tiles):
            og = jnp.concatenate([outs[2 * n], outs[2 * n + 1]], axis=0)
            o_ref[0, :, DIFF_WIDTH + t * LANES: DIFF_WIDTH + (t + 1) * LANES] = og.T.astype(BF16)


def _attn_call(dq, gq, dkx, gkx, dvx, gvx, dkc, gkc, dvc, gvc, lamv, subg):
    b, s, _ = dq.shape
    nq = s // Q_TILE
    full3 = lambda a: pl.BlockSpec((1,) + a.shape[1:], lambda bi, i: (bi, 0, 0))
    full4 = lambda a: pl.BlockSpec((1,) + a.shape[1:], lambda bi, i: (bi, 0, 0, 0))
    const = lambda bi, i: (0, 0)
    return pl.pallas_call(
        _attn_kernel,
        grid=(b, nq),
        in_specs=[pl.BlockSpec((1, Q_TILE, DIFF_WIDTH), lambda bi, i: (bi, i, 0)),
                  pl.BlockSpec((1, Q_TILE, GQA_WIDTH), lambda bi, i: (bi, i, 0)),
                  full3(dkx), full3(gkx), full4(dvx), full4(gvx),
                  full3(dkc), full3(gkc), full4(dvc), full4(gvc),
                  pl.BlockSpec(lamv.shape, const), pl.BlockSpec(subg.shape, const)],
        out_specs=pl.BlockSpec((1, Q_TILE, DIFF_WIDTH + GQA_WIDTH), lambda bi, i: (bi, i, 0)),
        out_shape=jax.ShapeDtypeStruct((b, s, DIFF_WIDTH + GQA_WIDTH), BF16),
        compiler_params=_cparams(("parallel", "arbitrary")),
        name="attn",
    )(dq, gq, dkx, gkx, dvx, gvx, dkc, gkc, dvc, gvc, lamv, subg)


def _out_kernel(o_ref, w_ref, x_ref, mod_ref, g2_ref, x1_ref, h2_ref):
    attn = jnp.dot(o_ref[0], w_ref[...], preferred_element_type=F32)
    x1 = x_ref[0] + mod_ref[0, 2:3, :] * attn
    x1_ref[0] = x1
    ms = jnp.mean(x1 * x1, axis=-1, keepdims=True)
    y = x1 * lax.rsqrt(ms + NORM_EPS) * g2_ref[...]
    h2_ref[0] = (y * (1.0 + mod_ref[0, 4:5, :]) + mod_ref[0, 3:4, :]).astype(BF16)


def _out_call(o, w_out, x, mod3, g2):
    b, s, d = x.shape
    const = lambda bi, i: (0, 0)
    tile = lambda w: pl.BlockSpec((1, OUT_TILE, w), lambda bi, i: (bi, i, 0))
    return pl.pallas_call(
        _out_kernel,
        grid=(b, s // OUT_TILE),
        in_specs=[tile(o.shape[2]), pl.BlockSpec(w_out.shape, const), tile(d),
                  pl.BlockSpec((1, 6, d), lambda bi, i: (bi, 0, 0)), pl.BlockSpec((1, d), const)],
        out_specs=[tile(d), tile(d)],
        out_shape=[jax.ShapeDtypeStruct((b, s, d), F32), jax.ShapeDtypeStruct((b, s, d), BF16)],
        compiler_params=_cparams(("parallel", "parallel")),
        name="outproj",
    )(o, w_out, x, mod3, g2)


def _topk_sublanes(s, flat, k):
    t = s.shape[1]
    big = jnp.iinfo(jnp.int32).max
    rowk = lax.broadcasted_iota(I32, (k, t), 0)
    vals = jnp.zeros((k, t), F32)
    labs = jnp.zeros((k, t), I32)
    for r in range(k):
        m = jnp.max(s, axis=0, keepdims=True)
        lab = jnp.min(jnp.where(s == m, flat, big), axis=0, keepdims=True)
        vals = jnp.where(rowk == r, m, vals)
        labs = jnp.where(rowk == r, lab, labs)
        s = jnp.where(flat == lab, -jnp.inf, s)
    return vals, labs


def _sel_kernel(h_ref, wqt_ref, sk_ref, ids_ref, gates_ref, qt_ref, idst_ref):
    ts = h_ref.shape[0]
    k = PEER_TOPK
    qt_ref[...] = lax.dot_general(wqt_ref[...], h_ref[...], (((1,), (1,)), ((), ())),
                                  preferred_element_type=F32).astype(BF16)
    key_iota = lax.broadcasted_iota(I32, (PEER_N_KEYS, ts), 0)
    sub8 = lax.broadcasted_iota(I32, (8, ts), 0)
    flat_rows = [lax.broadcasted_iota(I32, (k, ts), 0)] + [i * k + sub8 for i in range(1, k)]
    cand_flat = jnp.concatenate(flat_rows, axis=0)

    def head(h, carry):
        r0 = pl.multiple_of(h * 2 * PEER_HALF, 2 * PEER_HALF)
        tops = []
        for p in range(2):
            q = qt_ref[pl.ds(r0 + p * PEER_HALF, PEER_HALF), :]
            s = jnp.dot(sk_ref[2 * h + p], q, preferred_element_type=F32)
            tops.append(_topk_sublanes(s, key_iota, k))
        (v1, i1), (v2, i2) = tops
        vals = [v1[0:1] + v2] + [v1[i:i + 1] + v2[:8] for i in range(1, k)]
        eids = [i1[0:1] * PEER_N_KEYS + i2] + [i1[i:i + 1] * PEER_N_KEYS + i2[:8] for i in range(1, k)]
        cand = jnp.concatenate(vals, axis=0)
        cid = jnp.concatenate(eids, axis=0)
        best, pos = _topk_sublanes(cand, cand_flat, k)
        rowk = lax.broadcasted_iota(I32, (k, ts), 0)
        ids = jnp.zeros((k, ts), I32)
        for r in range(k):
            pick = jnp.sum(jnp.where(cand_flat == pos[r:r + 1], cid, 0), axis=0, keepdims=True)
            ids = jnp.where(rowk == r, pick, ids)
        e = jnp.exp(best - best[0:1])
        o0 = pl.multiple_of(h * k, k)
        gates_ref[pl.ds(o0, k), :] = e / jnp.sum(e, axis=0, keepdims=True)
        idst_ref[pl.ds(o0, k), :] = ids
        return carry

    lax.fori_loop(0, PEER_HEADS, head, 0)
    ids_ref[...] = idst_ref[...].T


def _sel_call(h2, wqt, sk):
    t, d = h2.shape
    nsel = PEER_HEADS * PEER_TOPK
    return pl.pallas_call(
        _sel_kernel,
        grid=(t // SEL_TILE,),
        in_specs=[pl.BlockSpec((SEL_TILE, d), lambda i: (i, 0)),
                  pl.BlockSpec(wqt.shape, lambda i: (0, 0)),
                  pl.BlockSpec(sk.shape, lambda i: (0, 0, 0))],
        out_specs=[pl.BlockSpec((SEL_TILE, nsel), lambda i: (i, 0)),
                   pl.BlockSpec((nsel, SEL_TILE), lambda i: (0, i))],
        out_shape=[jax.ShapeDtypeStruct((t, nsel), I32), jax.ShapeDtypeStruct((nsel, t), F32)],
        scratch_shapes=[pltpu.VMEM((wqt.shape[0], SEL_TILE), BF16), pltpu.VMEM((nsel, SEL_TILE), I32)],
        compiler_params=_cparams(("parallel",)),
        name="peer_select",
    )(h2, wqt, sk)


def _peer_kernel(ids_next_ref, ids_first_ref, gates_ref, h_ref, x1_ref, mod_ref, fg_ref, uv_hbm, o_ref,
                 gbuf, sem, mix_ref, *, nsteps):
    tt = h_ref.shape[0]
    nsel = gates_ref.shape[1]
    d = h_ref.shape[1]
    i = pl.program_id(0)
    slot = lax.rem(i, 2)

    def row_copy(ids_ref, t, e, sl):
        return pltpu.make_async_copy(uv_hbm.at[ids_ref[t, e]], gbuf.at[sl, pl.ds(t * nsel + e, 1)], sem.at[sl])

    def wait_slot(sl):
        pltpu.make_async_copy(gbuf.at[1 - sl], gbuf.at[sl], sem.at[sl]).wait()

    @pl.when(i == 0)
    def _():
        def body(t, c):
            for e in range(nsel):
                row_copy(ids_first_ref, t, e, 0).start(priority=e % 2)
            return c
        lax.fori_loop(0, tt, body, 0)

    for t in range(tt):
        for e in range(nsel):
            row_copy(ids_next_ref, t, e, 1 - slot).start(priority=e % 2)

    wait_slot(slot)

    hx = h_ref[...].astype(F32)
    lane_t = lax.broadcasted_iota(I32, (nsel, tt), 1)
    a = jnp.zeros((nsel, tt), F32)
    for t in range(tt):
        u = gbuf[slot, t * nsel:(t + 1) * nsel, 0:d]
        prod = u * hx[t:t + 1, :]
        part = prod[:, 0:LANES]
        for c in range(1, d // LANES):
            part = part + prod[:, c * LANES:(c + 1) * LANES]
        a = jnp.where(lane_t == t, jnp.sum(part, axis=1, keepdims=True), a)
    gelu = 0.5 * a * (1.0 + lax.erf(a * (2.0 ** -0.5)))
    w = gates_ref[0] * gelu
    for t in range(tt):
        v = gbuf[slot, t * nsel:(t + 1) * nsel, d:2 * d]
        mix_ref[t:t + 1, :] = jnp.sum(v * w[:, t:t + 1], axis=0, keepdims=True)
    y = x1_ref[...] + mod_ref[0, 5:6, :] * mix_ref[...]
    ms = jnp.mean(y * y, axis=-1, keepdims=True)
    o_ref[...] = y * lax.rsqrt(ms + NORM_EPS) * fg_ref[...]

    @pl.when(i == nsteps - 1)
    def _():
        wait_slot(1 - slot)


def _peer_call(ids, gates_r, h2, x1, mod3, fg, uv, seq):
    t, d = h2.shape
    nsel = ids.shape[1]
    tt = PEER_TILE
    nsteps = t // tt
    per_batch = seq // tt
    return pl.pallas_call(
        functools.partial(_peer_kernel, nsteps=nsteps),
        grid=(nsteps,),
        in_specs=[pl.BlockSpec((tt, nsel), lambda i: (jnp.minimum(i + 1, nsteps - 1), 0), memory_space=pltpu.SMEM),
                  pl.BlockSpec((tt, nsel), lambda i: (0, 0), memory_space=pltpu.SMEM),
                  pl.BlockSpec((1, nsel, tt), lambda i: (i, 0, 0)),
                  pl.BlockSpec((tt, d), lambda i: (i, 0)),
                  pl.BlockSpec((tt, d), lambda i: (i, 0)),
                  pl.BlockSpec((1, 6, d), lambda i: (i // per_batch, 0, 0)),
                  pl.BlockSpec((1, d), lambda i: (0, 0)),
                  pl.BlockSpec(memory_space=pl.ANY)],
        out_specs=pl.BlockSpec((tt, d), lambda i: (i, 0)),
        out_shape=jax.ShapeDtypeStruct((t, d), F32),
        scratch_shapes=[pltpu.VMEM((2, tt * nsel, 2 * d), F32),
                        pltpu.SemaphoreType.DMA((2,)),
                        pltpu.VMEM((tt, d), F32)],
        compiler_params=_cparams(("arbitrary",)),
        name="peer_mix",
    )(ids, ids, gates_r, h2, x1, mod3, fg, uv)


def _rope_tables(seq):
    half = HEAD_DIM // 4
    freqs = ROPE_THETA ** (-jnp.arange(half, dtype=F32) / half)
    pos = jnp.arange(seq, dtype=jnp.int32)
    ang_r = (pos // GRID_W).astype(F32)[:, None] * freqs[None, :]
    ang_c = (pos % GRID_W).astype(F32)[:, None] * freqs[None, :]
    z = jnp.zeros_like(ang_r)
    cos = jnp.concatenate([jnp.cos(ang_r)] * 2 + [jnp.cos(ang_c)] * 2, axis=1)
    sa = jnp.concatenate([-jnp.sin(ang_r), z, -jnp.sin(ang_c), z], axis=1)
    sb = jnp.concatenate([z, jnp.sin(ang_r), z, jnp.sin(ang_c)], axis=1)
    rep = LANES // HEAD_DIM
    return tuple(jnp.tile(a, (1, rep)) for a in (cos, sa, sb))


def kernel(x, c, ctx, c_ctx, w_mod, b_mod, norm1_g, norm2_g, w_in, w_out, diff_lq1, diff_lk1, diff_lq2,
           diff_lk2, diff_subln_g, gqa_q_norm_g, gqa_k_norm_g, peer_wq, peer_subkeys, peer_u, peer_v,
           final_norm_g):
    b, s, d = x.shape
    assert w_mod.shape[0] == 1, "depth-1 block"
    assert s % IN_TILE == 0 and s % Q_TILE == 0 and ctx.shape[1] % LANES == 0

    pad = (-(b + 1)) % 8
    cc = jnp.concatenate([c, c_ctx[None, :], jnp.zeros((pad, d), F32)], axis=0)
    mod3 = _mod_call(cc, w_mod[0], b_mod[0][None, :]).reshape(cc.shape[0], 6, d)

    w0 = w_in[0]
    o_dk, o_dv, o_gq, o_gk, o_gv = (DIFF_WIDTH, 2 * DIFF_WIDTH, 3 * DIFF_WIDTH, 3 * DIFF_WIDTH + GQA_WIDTH,
                                    3 * DIFF_WIDTH + GQA_WIDTH + GQA_KV_WIDTH)
    group = GQA_Q_HEADS // GQA_KV_HEADS
    head_order = [j * group + t for t in range(group) for j in range(GQA_KV_HEADS)]
    gq_cols = jnp.asarray([hd * HEAD_DIM + e for hd in head_order for e in range(HEAD_DIM)], dtype=jnp.int32)
    w_gq = w0[:, o_gq:o_gk][:, gq_cols]
    w_x = jnp.concatenate([w0[:, :o_gq], w_gq, w0[:, o_gk:]], axis=1).astype(BF16)
    w_c = jnp.concatenate([w0[:, o_dk:o_gq], w0[:, o_gk:]], axis=1).astype(BF16)
    w_o = jnp.concatenate([w_out[0][:DIFF_WIDTH], w_out[0][DIFF_WIDTH:][gq_cols]], axis=0).astype(BF16)

    seg = jnp.arange(GQA_WIDTH) // HEAD_DIM
    bd = (seg[:, None] == seg[None, :]).astype(BF16)
    gqg = jnp.tile(gqa_q_norm_g[0], GQA_Q_HEADS)[None, :]
    gkg = jnp.tile(gqa_k_norm_g[0], GQA_KV_HEADS)[None, :]
    g1 = norm1_g[0][None, :]
    tables = _rope_tables(s)

    dq, gq, dkx, gkx, dvx, gvx = _inproj_call(x, mod3, lambda bi, i: (bi, 0, 0), g1, w_x, tables, bd, gqg, gkg,
                                              with_q=True, tile=IN_TILE)
    ctx_tile = ctx.shape[1]
    ctx_tables = tuple(a[:ctx_tile] for a in tables)
    dkc, gkc, dvc, gvc = _inproj_call(ctx, mod3, lambda bi, i: (b, 0, 0), g1, w_c, ctx_tables, bd, gqg, gkg,
                                      with_q=False, tile=ctx_tile)

    lamv = jnp.stack([diff_lq1[0], diff_lk1[0], diff_lq2[0], diff_lk2[0]], axis=0).astype(F32)
    subg = diff_subln_g[0][:, None]
    o = _attn_call(dq, gq, dkx, gkx, dvx, gvx, dkc, gkc, dvc, gvc, lamv, subg)

    x1, h2 = _out_call(o, w_o, x, mod3, norm2_g[0][None, :])

    t = b * s
    h2f = h2.reshape(t, d)
    wqt = peer_wq[0].T.astype(BF16)
    sk = peer_subkeys[0].reshape(PEER_HEADS * 2, PEER_N_KEYS, PEER_HALF).astype(BF16)
    ids, gates_t = _sel_call(h2f, wqt, sk)

    nsel = ids.shape[1]
    gates_r = gates_t.reshape(nsel, t // PEER_TILE, PEER_TILE).transpose(1, 0, 2)
    uv = jnp.concatenate([peer_u[0], peer_v[0]], axis=1)[:, None, :]
    out = _peer_call(ids, gates_r, h2f, x1.reshape(t, d), mod3, final_norm_g[None, :], uv, s)
    return out.reshape(b, s, d)
```

```python
import functools
import math

import jax
import jax.numpy as jnp
from jax import lax
from jax.experimental import pallas as pl
from jax.experimental.pallas import tpu as pltpu

F32 = jnp.float32
BF16 = jnp.bfloat16
I32 = jnp.int32

HEAD_DIM = 64
GRID_W = 64
ROPE_THETA = 10000.0
NORM_EPS = 1e-6
DIFF_HEADS = 4
DIFF_V_DIM = 2 * HEAD_DIM
DIFF_WIDTH = DIFF_HEADS * DIFF_V_DIM
GQA_Q_HEADS = 8
GQA_KV_HEADS = 2
GQA_WIDTH = GQA_Q_HEADS * HEAD_DIM
GQA_KV_WIDTH = GQA_KV_HEADS * HEAD_DIM
PEER_HEADS = 8
PEER_N_KEYS = 128
PEER_HALF = 128
PEER_TOPK = 16
LAMBDA_INIT = 0.8 - 0.6 * math.exp(-0.3 * 0)
LANES = 128
VMEM_LIMIT = 56 * 1024 * 1024

IN_TILE = 512
Q_TILE = 256
OUT_TILE = 256
SEL_TILE = 256
PEER_TILE = 16
ATTN_GROUP = 8


def _cparams(sem):
    return pltpu.CompilerParams(dimension_semantics=sem, vmem_limit_bytes=VMEM_LIMIT)


def _split_bf16(a):
    hi = a.astype(BF16)
    lo = (a - hi.astype(F32)).astype(BF16)
    return hi, lo


def _mod_kernel(c_ref, w_ref, b_ref, o_ref):
    s = jax.nn.silu(c_ref[...])
    s_hi, s_lo = _split_bf16(s)
    w_hi, w_lo = _split_bf16(w_ref[...])
    acc = jnp.dot(s_hi, w_hi, preferred_element_type=F32)
    acc += jnp.dot(s_hi, w_lo, preferred_element_type=F32)
    acc += jnp.dot(s_lo, w_hi, preferred_element_type=F32)
    o_ref[...] = acc + b_ref[...]


def _mod_call(cc, w_mod, b_mod):
    rows, d = cc.shape
    n = w_mod.shape[1]
    tn = n // 4
    return pl.pallas_call(
        _mod_kernel,
        grid=(n // tn,),
        in_specs=[pl.BlockSpec((rows, d), lambda j: (0, 0)),
                  pl.BlockSpec((d, tn), lambda j: (0, j)),
                  pl.BlockSpec((1, tn), lambda j: (0, j))],
        out_specs=pl.BlockSpec((rows, tn), lambda j: (0, j)),
        out_shape=jax.ShapeDtypeStruct((rows, n), F32),
        compiler_params=_cparams(("arbitrary",)),
        name="mod",
    )(cc, w_mod, b_mod)


def _rope_tile(x, cos, sa, sb):
    return x * cos + pltpu.roll(x, LANES - 16, 1) * sa + pltpu.roll(x, 16, 1) * sb


def _head_rms(x, bd, g):
    sq = x * x
    hi, lo = _split_bf16(sq)
    ssum = jnp.dot(hi, bd, preferred_element_type=F32) + jnp.dot(lo, bd, preferred_element_type=F32)
    return x * lax.rsqrt(ssum * (1.0 / HEAD_DIM) + NORM_EPS) * g


def _inproj_kernel(x_ref, mod_ref, g1_ref, w_ref, cos_ref, sa_ref, sb_ref, bd_ref, gqg_ref, gkg_ref,
                   *out_refs, with_q, with_rope):
    x = x_ref[0]
    ms = jnp.mean(x * x, axis=-1, keepdims=True)
    y = x * lax.rsqrt(ms + NORM_EPS) * g1_ref[...]
    h = (y * (1.0 + mod_ref[0, 1:2, :]) + mod_ref[0, 0:1, :]).astype(BF16)
    p = jnp.dot(h, w_ref[...], preferred_element_type=F32)

    if with_rope:
        cos, sa, sb = cos_ref[...], sa_ref[...], sb_ref[...]
        rope = lambda t: _rope_tile(t, cos, sa, sb)
    else:
        rope = lambda t: t
    bd = bd_ref[...]
    scale = HEAD_DIM ** -0.5

    if with_q:
        dq_ref, gq_ref, dk_ref, gk_ref, dvt_ref, gvt_ref = out_refs
        off = 0
        for t in range(DIFF_WIDTH // LANES):
            sl = slice(t * LANES, (t + 1) * LANES)
            dq_ref[0, :, sl] = (rope(p[:, off + t * LANES: off + (t + 1) * LANES]) * scale).astype(BF16)
        off += DIFF_WIDTH
    else:
        dk_ref, gk_ref, dvt_ref, gvt_ref = out_refs
        off = 0
    for t in range(DIFF_WIDTH // LANES):
        sl = slice(t * LANES, (t + 1) * LANES)
        dk_ref[0, :, sl] = rope(p[:, off + t * LANES: off + (t + 1) * LANES]).astype(BF16)
    off += DIFF_WIDTH
    dvt_ref[0, 0] = p[:, off: off + DIFF_WIDTH].T.astype(BF16)
    off += DIFF_WIDTH
    if with_q:
        gqn = _head_rms(p[:, off: off + GQA_WIDTH], bd, gqg_ref[...])
        for t in range(GQA_WIDTH // LANES):
            sl = slice(t * LANES, (t + 1) * LANES)
            gq_ref[0, :, sl] = (rope(gqn[:, sl]) * scale).astype(BF16)
        off += GQA_WIDTH
    gkn = _head_rms(p[:, off: off + GQA_KV_WIDTH], bd[:GQA_KV_WIDTH, :GQA_KV_WIDTH], gkg_ref[...])
    gk_ref[0] = rope(gkn).astype(BF16)
    off += GQA_KV_WIDTH
    gvt_ref[0, 0] = p[:, off: off + GQA_KV_WIDTH].T.astype(BF16)


def _inproj_call(x, mod3, mod_row, g1, w, tables, bd, gqg, gkg, *, with_q, tile):
    b, s, d = x.shape
    n = w.shape[1]
    nt = s // tile
    cos, sa, sb = tables
    kv_shapes = [jax.ShapeDtypeStruct((b, s, DIFF_WIDTH), BF16),
                 jax.ShapeDtypeStruct((b, s, GQA_KV_WIDTH), BF16),
                 jax.ShapeDtypeStruct((b, nt, DIFF_WIDTH, tile), BF16),
                 jax.ShapeDtypeStruct((b, nt, GQA_KV_WIDTH, tile), BF16)]
    kv_specs = [pl.BlockSpec((1, tile, DIFF_WIDTH), lambda bi, i: (bi, i, 0)),
                pl.BlockSpec((1, tile, GQA_KV_WIDTH), lambda bi, i: (bi, i, 0)),
                pl.BlockSpec((1, 1, DIFF_WIDTH, tile), lambda bi, i: (bi, i, 0, 0)),
                pl.BlockSpec((1, 1, GQA_KV_WIDTH, tile), lambda bi, i: (bi, i, 0, 0))]
    if with_q:
        out_shapes = [jax.ShapeDtypeStruct((b, s, DIFF_WIDTH), BF16),
                      jax.ShapeDtypeStruct((b, s, GQA_WIDTH), BF16)] + kv_shapes
        out_specs = [pl.BlockSpec((1, tile, DIFF_WIDTH), lambda bi, i: (bi, i, 0)),
                     pl.BlockSpec((1, tile, GQA_WIDTH), lambda bi, i: (bi, i, 0))] + kv_specs
    else:
        out_shapes, out_specs = kv_shapes, kv_specs
    const = lambda bi, i: (0, 0)
    return pl.pallas_call(
        functools.partial(_inproj_kernel, with_q=with_q, with_rope=with_q),
        grid=(b, nt),
        in_specs=[pl.BlockSpec((1, tile, d), lambda bi, i: (bi, i, 0)),
                  pl.BlockSpec((1, 6, d), mod_row),
                  pl.BlockSpec((1, d), const),
                  pl.BlockSpec((d, n), const),
                  pl.BlockSpec((tile, LANES), lambda bi, i: (i, 0)),
                  pl.BlockSpec((tile, LANES), lambda bi, i: (i, 0)),
                  pl.BlockSpec((tile, LANES), lambda bi, i: (i, 0)),
                  pl.BlockSpec(bd.shape, const),
                  pl.BlockSpec(gqg.shape, const),
                  pl.BlockSpec(gkg.shape, const)],
        out_specs=out_specs,
        out_shape=out_shapes,
        compiler_params=_cparams(("parallel", "parallel")),
        name="inproj_x" if with_q else "inproj_ctx",
    )(x, mod3, g1, w, cos, sa, sb, bd, gqg, gkg)


def _attn_steps(ks, qs, vts, carries):
    ss = [lax.dot_general(k, q, (((1,), (1,)), ((), ())), preferred_element_type=F32) for k, q in zip(ks, qs)]
    stats = []
    for s, (m, l, _) in zip(ss, carries):
        m_new = jnp.maximum(m, jnp.max(s, axis=0, keepdims=True))
        alpha = jnp.exp(m - m_new)
        p = jnp.exp(s - m_new)
        stats.append((m_new, alpha, alpha * l + jnp.sum(p, axis=0, keepdims=True), p.astype(BF16)))
    return tuple((m_new, l, alpha * acc + jnp.dot(vt, p, preferred_element_type=F32))
                 for (m_new, alpha, l, p), vt, (_, _, acc) in zip(stats, vts, carries))


def _attn_kernel(dq_ref, gq_ref, dkx_ref, gkx_ref, dvx_ref, gvx_ref, dkc_ref, gkc_ref, dvc_ref, gvc_ref,
                 lamv_ref, subg_ref, o_ref):
    tq = dq_ref.shape[1]
    n_chunks, chunk = dvx_ref.shape[1], dvx_ref.shape[3]
    lv = lamv_ref[...]
    lam = (jnp.exp(jnp.sum(lv[0:1] * lv[1:2], axis=-1, keepdims=True))
           - jnp.exp(jnp.sum(lv[2:3] * lv[3:4], axis=-1, keepdims=True)) + LAMBDA_INIT)
    low = lax.broadcasted_iota(I32, (tq, LANES), 1) < HEAD_DIM

    def run_units(units):
        def go(kx_ref, kc_ref, vx_ref, vc_ref, dv):
            qs = [q for q, _, _ in units]
            init = tuple((jnp.full((1, tq), -jnp.inf, F32), jnp.zeros((1, tq), F32), jnp.zeros((dv, tq), F32))
                         for _ in units)
            carries = _attn_steps([kc_ref[0, :, col:col + LANES] for _, col, _ in units], qs,
                                  [vc_ref[0, 0, r0:r0 + dv, :] for _, _, r0 in units], init)

            def body(c, carries):
                off = pl.multiple_of(c * chunk, chunk)
                return _attn_steps([kx_ref[0, pl.ds(off, chunk), col:col + LANES] for _, col, _ in units], qs,
                                   [vx_ref[0, c, r0:r0 + dv, :] for _, _, r0 in units], carries)

            return [acc / l for _, l, acc in lax.fori_loop(0, n_chunks, body, carries)]
        return go

    zero = jnp.zeros((tq, LANES), BF16)
    for h0 in range(0, DIFF_HEADS, ATTN_GROUP // 2):
        heads = range(h0, h0 + ATTN_GROUP // 2)
        units = []
        for h in heads:
            qh = dq_ref[0, :, h * LANES:(h + 1) * LANES]
            units += [(jnp.where(low, qh, zero), h * LANES, h * DIFF_V_DIM),
                      (jnp.where(low, zero, qh), h * LANES, h * DIFF_V_DIM)]
        outs = run_units(units)(dkx_ref, dkc_ref, dvx_ref, dvc_ref, DIFF_V_DIM)
        for n, h in enumerate(heads):
            od = outs[2 * n] - lam * outs[2 * n + 1]
            ms = jnp.mean(od * od, axis=0, keepdims=True)
            od = od * lax.rsqrt(ms + NORM_EPS) * subg_ref[...] * (1.0 - LAMBDA_INIT)
            o_ref[0, :, h * LANES:(h + 1) * LANES] = od.T.astype(BF16)
    for t0 in range(0, GQA_Q_HEADS // 2, ATTN_GROUP // 2):
        tiles = range(t0, t0 + ATTN_GROUP // 2)
        units = []
        for t in tiles:
            qt = gq_ref[0, :, t * LANES:(t + 1) * LANES]
            units += [(jnp.where(low, qt, zero), 0, 0), (jnp.where(low, zero, qt), 0, HEAD_DIM)]
        outs = run_units(units)(gkx_ref, gkc_ref, gvx_ref, gvc_ref, HEAD_DIM)
        for n, t in enumerate(tiles):
            og = jnp.concatenate([outs[2 * n], outs[2 * n + 1]], axis=0)
            o_ref[0, :, DIFF_WIDTH + t * LANES: DIFF_WIDTH + (t + 1) * LANES] = og.T.astype(BF16)


def _attn_call(dq, gq, dkx, gkx, dvx, gvx, dkc, gkc, dvc, gvc, lamv, subg):
    b, s, _ = dq.shape
    nq = s // Q_TILE
    full3 = lambda a: pl.BlockSpec((1,) + a.shape[1:], lambda bi, i: (bi, 0, 0))
    full4 = lambda a: pl.BlockSpec((1,) + a.shape[1:], lambda bi, i: (bi, 0, 0, 0))
    const = lambda bi, i: (0, 0)
    return pl.pallas_call(
        _attn_kernel,
        grid=(b, nq),
        in_specs=[pl.BlockSpec((1, Q_TILE, DIFF_WIDTH), lambda bi, i: (bi, i, 0)),
                  pl.BlockSpec((1, Q_TILE, GQA_WIDTH), lambda bi, i: (bi, i, 0)),
                  full3(dkx), full3(gkx), full4(dvx), full4(gvx),
                  full3(dkc), full3(gkc), full4(dvc), full4(gvc),
                  pl.BlockSpec(lamv.shape, const), pl.BlockSpec(subg.shape, const)],
        out_specs=pl.BlockSpec((1, Q_TILE, DIFF_WIDTH + GQA_WIDTH), lambda bi, i: (bi, i, 0)),
        out_shape=jax.ShapeDtypeStruct((b, s, DIFF_WIDTH + GQA_WIDTH), BF16),
        compiler_params=_cparams(("parallel", "arbitrary")),
        name="attn",
    )(dq, gq, dkx, gkx, dvx, gvx, dkc, gkc, dvc, gvc, lamv, subg)


def _out_kernel(o_ref, w_ref, x_ref, mod_ref, g2_ref, x1_ref, h2_ref):
    attn = jnp.dot(o_ref[0], w_ref[...], preferred_element_type=F32)
    x1 = x_ref[0] + mod_ref[0, 2:3, :] * attn
    x1_ref[0] = x1
    ms = jnp.mean(x1 * x1, axis=-1, keepdims=True)
    y = x1 * lax.rsqrt(ms + NORM_EPS) * g2_ref[...]
    h2_ref[0] = (y * (1.0 + mod_ref[0, 4:5, :]) + mod_ref[0, 3:4, :]).astype(BF16)


def _out_call(o, w_out, x, mod3, g2):
    b, s, d = x.shape
    const = lambda bi, i: (0, 0)
    tile = lambda w: pl.BlockSpec((1, OUT_TILE, w), lambda bi, i: (bi, i, 0))
    return pl.pallas_call(
        _out_kernel,
        grid=(b, s // OUT_TILE),
        in_specs=[tile(o.shape[2]), pl.BlockSpec(w_out.shape, const), tile(d),
                  pl.BlockSpec((1, 6, d), lambda bi, i: (bi, 0, 0)), pl.BlockSpec((1, d), const)],
        out_specs=[tile(d), tile(d)],
        out_shape=[jax.ShapeDtypeStruct((b, s, d), F32), jax.ShapeDtypeStruct((b, s, d), BF16)],
        compiler_params=_cparams(("parallel", "parallel")),
        name="outproj",
    )(o, w_out, x, mod3, g2)


def _topk_sublanes(s, flat, k):
    t = s.shape[1]
    big = jnp.iinfo(jnp.int32).max
    rowk = lax.broadcasted_iota(I32, (k, t), 0)
    vals = jnp.zeros((k, t), F32)
    labs = jnp.zeros((k, t), I32)
    for r in range(k):
        m = jnp.max(s, axis=0, keepdims=True)
        lab = jnp.min(jnp.where(s == m, flat, big), axis=0, keepdims=True)
        vals = jnp.where(rowk == r, m, vals)
        labs = jnp.where(rowk == r, lab, labs)
        s = jnp.where(flat == lab, -jnp.inf, s)
    return vals, labs


def _sel_kernel(h_ref, wqt_ref, sk_ref, ids_ref, gates_ref, qt_ref, idst_ref):
    ts = h_ref.shape[0]
    k = PEER_TOPK
    qt_ref[...] = lax.dot_general(wqt_ref[...], h_ref[...], (((1,), (1,)), ((), ())),
                                  preferred_element_type=F32).astype(BF16)
    key_iota = lax.broadcasted_iota(I32, (PEER_N_KEYS, ts), 0)
    sub8 = lax.broadcasted_iota(I32, (8, ts), 0)
    flat_rows = [lax.broadcasted_iota(I32, (k, ts), 0)] + [i * k + sub8 for i in range(1, k)]
    cand_flat = jnp.concatenate(flat_rows, axis=0)

    def head(h, carry):
        r0 = pl.multiple_of(h * 2 * PEER_HALF, 2 * PEER_HALF)
        tops = []
        for p in range(2):
            q = qt_ref[pl.ds(r0 + p * PEER_HALF, PEER_HALF), :]
            s = jnp.dot(sk_ref[2 * h + p], q, preferred_element_type=F32)
            tops.append(_topk_sublanes(s, key_iota, k))
        (v1, i1), (v2, i2) = tops
        vals = [v1[0:1] + v2] + [v1[i:i + 1] + v2[:8] for i in range(1, k)]
        eids = [i1[0:1] * PEER_N_KEYS + i2] + [i1[i:i + 1] * PEER_N_KEYS + i2[:8] for i in range(1, k)]
        cand = jnp.concatenate(vals, axis=0)
        cid = jnp.concatenate(eids, axis=0)
        best, pos = _topk_sublanes(cand, cand_flat, k)
        rowk = lax.broadcasted_iota(I32, (k, ts), 0)
        ids = jnp.zeros((k, ts), I32)
        for r in range(k):
            pick = jnp.sum(jnp.where(cand_flat == pos[r:r + 1], cid, 0), axis=0, keepdims=True)
            ids = jnp.where(rowk == r, pick, ids)
        e = jnp.exp(best - best[0:1])
        o0 = pl.multiple_of(h * k, k)
        gates_ref[pl.ds(o0, k), :] = e / jnp.sum(e, axis=0, keepdims=True)
        idst_ref[pl.ds(o0, k), :] = ids
        return carry

    lax.fori_loop(0, PEER_HEADS, head, 0)
    ids_ref[...] = idst_ref[...].T


def _sel_call(h2, wqt, sk):
    t, d = h2.shape
    nsel = PEER_HEADS * PEER_TOPK
    return pl.pallas_call(
        _sel_kernel,
        grid=(t // SEL_TILE,),
        in_specs=[pl.BlockSpec((SEL_TILE, d), lambda i: (i, 0)),
                  pl.BlockSpec(wqt.shape, lambda i: (0, 0)),
                  pl.BlockSpec(sk.shape, lambda i: (0, 0, 0))],
        out_specs=[pl.BlockSpec((SEL_TILE, nsel), lambda i: (i, 0)),
                   pl.BlockSpec((nsel, SEL_TILE), lambda i: (0, i))],
        out_shape=[jax.ShapeDtypeStruct((t, nsel), I32), jax.ShapeDtypeStruct((nsel, t), F32)],
        scratch_shapes=[pltpu.VMEM((wqt.shape[0], SEL_TILE), BF16), pltpu.VMEM((nsel, SEL_TILE), I32)],
        compiler_params=_cparams(("parallel",)),
        name="peer_select",
    )(h2, wqt, sk)


def _peer_kernel(ids_cur_ref, ids_next_ref, gates_ref, h_ref, x1_ref, mod_ref, fg_ref, uv_hbm, o_ref,
                 buf0, buf1, sem, mix_ref, *, nsteps):
    tt = h_ref.shape[0] // 2
    nsel = gates_ref.shape[1]
    d = h_ref.shape[1]
    i = pl.program_id(0)
    bufs = (buf0, buf1)

    def row_copy(ids_ref, t_ids, t, e, sl):
        return pltpu.make_async_copy(uv_hbm.at[ids_ref[t_ids, e]], bufs[sl].at[pl.ds(t * nsel + e, 1)], sem.at[sl])

    def issue(ids_ref, t0, sl):
        for t in range(tt):
            for e in range(nsel):
                row_copy(ids_ref, t0 + t, t, e, sl).start(priority=e % 2)

    def wait_slot(sl):
        pltpu.make_async_copy(bufs[1 - sl], bufs[sl], sem.at[sl]).wait()

    def mix(sl, t0):
        buf = bufs[sl]
        hx = h_ref[t0:t0 + tt, :].astype(F32)
        lane_t = lax.broadcasted_iota(I32, (nsel, tt), 1)
        a = jnp.zeros((nsel, tt), F32)
        for t in range(tt):
            u = pltpu.bitcast(buf[t * nsel:(t + 1) * nsel, :] << 16, F32)
            prod = u * hx[t:t + 1, :]
            part = prod[:, 0:LANES]
            for c in range(1, d // LANES):
                part = part + prod[:, c * LANES:(c + 1) * LANES]
            a = jnp.where(lane_t == t, jnp.sum(part, axis=1, keepdims=True), a)
        gelu = 0.5 * a * (1.0 + lax.erf(a * (2.0 ** -0.5)))
        w = gates_ref[sl] * gelu
        hi_mask = jnp.uint32(0xFFFF0000)
        for t in range(tt):
            v = pltpu.bitcast(buf[t * nsel:(t + 1) * nsel, :] & hi_mask, F32)
            mix_ref[t0 + t:t0 + t + 1, :] = jnp.sum(v * w[:, t:t + 1], axis=0, keepdims=True)

    @pl.when(i == 0)
    def _():
        def body(t, c):
            for e in range(nsel):
                row_copy(ids_cur_ref, t, t, e, 0).start(priority=e % 2)
            return c
        lax.fori_loop(0, tt, body, 0)

    wait_slot(0)
    issue(ids_cur_ref, tt, 1)
    mix(0, 0)
    wait_slot(1)
    issue(ids_next_ref, 0, 0)
    mix(1, tt)

    y = x1_ref[...] + mod_ref[0, 5:6, :] * mix_ref[...]
    ms = jnp.mean(y * y, axis=-1, keepdims=True)
    o_ref[...] = y * lax.rsqrt(ms + NORM_EPS) * fg_ref[...]

    @pl.when(i == nsteps - 1)
    def _():
        wait_slot(0)


def _peer_call(ids, gates_r, h2, x1, mod3, fg, uv, seq):
    t, d = h2.shape
    nsel = ids.shape[1]
    tt = PEER_TILE
    nsteps = t // (2 * tt)
    per_batch = seq // (2 * tt)
    rows = lambda w: pl.BlockSpec((2 * tt, w), lambda i: (i, 0))
    return pl.pallas_call(
        functools.partial(_peer_kernel, nsteps=nsteps),
        grid=(nsteps,),
        in_specs=[pl.BlockSpec((2 * tt, nsel), lambda i: (i, 0), memory_space=pltpu.SMEM),
                  pl.BlockSpec((2 * tt, nsel), lambda i: (jnp.minimum(i + 1, nsteps - 1), 0),
                               memory_space=pltpu.SMEM),
                  pl.BlockSpec((2, nsel, tt), lambda i: (i, 0, 0)),
                  rows(d), rows(d),
                  pl.BlockSpec((1, 6, d), lambda i: (i // per_batch, 0, 0)),
                  pl.BlockSpec((1, d), lambda i: (0, 0)),
                  pl.BlockSpec(memory_space=pl.ANY)],
        out_specs=rows(d),
        out_shape=jax.ShapeDtypeStruct((t, d), F32),
        scratch_shapes=[pltpu.VMEM((tt * nsel, d), jnp.uint32),
                        pltpu.VMEM((tt * nsel, d), jnp.uint32),
                        pltpu.SemaphoreType.DMA((2,)),
                        pltpu.VMEM((2 * tt, d), F32)],
        compiler_params=_cparams(("arbitrary",)),
        name="peer_mix",
    )(ids, ids, gates_r, h2, x1, mod3, fg, uv)


def _rope_tables(seq):
    half = HEAD_DIM // 4
    freqs = ROPE_THETA ** (-jnp.arange(half, dtype=F32) / half)
    pos = jnp.arange(seq, dtype=jnp.int32)
    ang_r = (pos // GRID_W).astype(F32)[:, None] * freqs[None, :]
    ang_c = (pos % GRID_W).astype(F32)[:, None] * freqs[None, :]
    z = jnp.zeros_like(ang_r)
    cos = jnp.concatenate([jnp.cos(ang_r)] * 2 + [jnp.cos(ang_c)] * 2, axis=1)
    sa = jnp.concatenate([-jnp.sin(ang_r), z, -jnp.sin(ang_c), z], axis=1)
    sb = jnp.concatenate([z, jnp.sin(ang_r), z, jnp.sin(ang_c)], axis=1)
    rep = LANES // HEAD_DIM
    return tuple(jnp.tile(a, (1, rep)) for a in (cos, sa, sb))


def kernel(x, c, ctx, c_ctx, w_mod, b_mod, norm1_g, norm2_g, w_in, w_out, diff_lq1, diff_lk1, diff_lq2,
           diff_lk2, diff_subln_g, gqa_q_norm_g, gqa_k_norm_g, peer_wq, peer_subkeys, peer_u, peer_v,
           final_norm_g):
    b, s, d = x.shape
    assert w_mod.shape[0] == 1, "depth-1 block"
    assert s % IN_TILE == 0 and s % Q_TILE == 0 and ctx.shape[1] % LANES == 0

    pad = (-(b + 1)) % 8
    cc = jnp.concatenate([c, c_ctx[None, :], jnp.zeros((pad, d), F32)], axis=0)
    mod3 = _mod_call(cc, w_mod[0], b_mod[0][None, :]).reshape(cc.shape[0], 6, d)

    w0 = w_in[0]
    o_dk, o_dv, o_gq, o_gk, o_gv = (DIFF_WIDTH, 2 * DIFF_WIDTH, 3 * DIFF_WIDTH, 3 * DIFF_WIDTH + GQA_WIDTH,
                                    3 * DIFF_WIDTH + GQA_WIDTH + GQA_KV_WIDTH)
    group = GQA_Q_HEADS // GQA_KV_HEADS
    head_order = [j * group + t for t in range(group) for j in range(GQA_KV_HEADS)]
    gq_cols = jnp.asarray([hd * HEAD_DIM + e for hd in head_order for e in range(HEAD_DIM)], dtype=jnp.int32)
    w_gq = w0[:, o_gq:o_gk][:, gq_cols]
    w_x = jnp.concatenate([w0[:, :o_gq], w_gq, w0[:, o_gk:]], axis=1).astype(BF16)
    w_c = jnp.concatenate([w0[:, o_dk:o_gq], w0[:, o_gk:]], axis=1).astype(BF16)
    w_o = jnp.concatenate([w_out[0][:DIFF_WIDTH], w_out[0][DIFF_WIDTH:][gq_cols]], axis=0).astype(BF16)

    seg = jnp.arange(GQA_WIDTH) // HEAD_DIM
    bd = (seg[:, None] == seg[None, :]).astype(BF16)
    gqg = jnp.tile(gqa_q_norm_g[0], GQA_Q_HEADS)[None, :]
    gkg = jnp.tile(gqa_k_norm_g[0], GQA_KV_HEADS)[None, :]
    g1 = norm1_g[0][None, :]
    tables = _rope_tables(s)

    dq, gq, dkx, gkx, dvx, gvx = _inproj_call(x, mod3, lambda bi, i: (bi, 0, 0), g1, w_x, tables, bd, gqg, gkg,
                                              with_q=True, tile=IN_TILE)
    ctx_tile = ctx.shape[1]
    ctx_tables = tuple(a[:ctx_tile] for a in tables)
    dkc, gkc, dvc, gvc = _inproj_call(ctx, mod3, lambda bi, i: (b, 0, 0), g1, w_c, ctx_tables, bd, gqg, gkg,
                                      with_q=False, tile=ctx_tile)

    lamv = jnp.stack([diff_lq1[0], diff_lk1[0], diff_lq2[0], diff_lk2[0]], axis=0).astype(F32)
    subg = diff_subln_g[0][:, None]
    o = _attn_call(dq, gq, dkx, gkx, dvx, gvx, dkc, gkc, dvc, gvc, lamv, subg)

    x1, h2 = _out_call(o, w_o, x, mod3, norm2_g[0][None, :])

    t = b * s
    h2f = h2.reshape(t, d)
    wqt = peer_wq[0].T.astype(BF16)
    sk = peer_subkeys[0].reshape(PEER_HEADS * 2, PEER_N_KEYS, PEER_HALF).astype(BF16)
    ids, gates_t = _sel_call(h2f, wqt, sk)

    nsel = ids.shape[1]
    gates_r = gates_t.reshape(nsel, t // PEER_TILE, PEER_TILE).transpose(1, 0, 2)
    half_bits = lambda a: lax.bitcast_convert_type(a.astype(BF16), jnp.uint16).astype(jnp.uint32)
    uv = (half_bits(peer_u[0]) | (half_bits(peer_v[0]) << 16))[:, None, :]
    out = _peer_call(ids, gates_r, h2f, x1.reshape(t, d), mod3, final_norm_g[None, :], uv, s)
    return out.reshape(b, s, d)
```

```python
import functools
import math

import jax
import jax.numpy as jnp
from jax import lax
from jax.experimental import pallas as pl
from jax.experimental.pallas import tpu as pltpu

F32 = jnp.float32
BF16 = jnp.bfloat16
I32 = jnp.int32

HEAD_DIM = 64
GRID_W = 64
ROPE_THETA = 10000.0
NORM_EPS = 1e-6
DIFF_HEADS = 4
DIFF_V_DIM = 2 * HEAD_DIM
DIFF_WIDTH = DIFF_HEADS * DIFF_V_DIM
GQA_Q_HEADS = 8
GQA_KV_HEADS = 2
GQA_WIDTH = GQA_Q_HEADS * HEAD_DIM
GQA_KV_WIDTH = GQA_KV_HEADS * HEAD_DIM
PEER_HEADS = 8
PEER_N_KEYS = 128
PEER_HALF = 128
PEER_TOPK = 16
LAMBDA_INIT = 0.8 - 0.6 * math.exp(-0.3 * 0)
LANES = 128
VMEM_LIMIT = 56 * 1024 * 1024

IN_TILE = 512
Q_TILE = 256
OUT_TILE = 256
SEL_TILE = 128
PEER_TILE = 16
PEER_GROUP = SEL_TILE // PEER_TILE
ATTN_GROUP = 8


def _cparams(sem):
    return pltpu.CompilerParams(dimension_semantics=sem, vmem_limit_bytes=VMEM_LIMIT)


def _split_bf16(a):
    hi = a.astype(BF16)
    lo = (a - hi.astype(F32)).astype(BF16)
    return hi, lo


def _mod_kernel(c_ref, w_ref, b_ref, o_ref):
    s = jax.nn.silu(c_ref[...])
    s_hi, s_lo = _split_bf16(s)
    w_hi, w_lo = _split_bf16(w_ref[...])
    acc = jnp.dot(s_hi, w_hi, preferred_element_type=F32)
    acc += jnp.dot(s_hi, w_lo, preferred_element_type=F32)
    acc += jnp.dot(s_lo, w_hi, preferred_element_type=F32)
    o_ref[...] = acc + b_ref[...]


def _mod_call(cc, w_mod, b_mod):
    rows, d = cc.shape
    n = w_mod.shape[1]
    tn = n // 4
    return pl.pallas_call(
        _mod_kernel,
        grid=(n // tn,),
        in_specs=[pl.BlockSpec((rows, d), lambda j: (0, 0)),
                  pl.BlockSpec((d, tn), lambda j: (0, j)),
                  pl.BlockSpec((1, tn), lambda j: (0, j))],
        out_specs=pl.BlockSpec((rows, tn), lambda j: (0, j)),
        out_shape=jax.ShapeDtypeStruct((rows, n), F32),
        compiler_params=_cparams(("arbitrary",)),
        name="mod",
    )(cc, w_mod, b_mod)


def _rope_tile(x, cos, sa, sb):
    return x * cos + pltpu.roll(x, LANES - 16, 1) * sa + pltpu.roll(x, 16, 1) * sb


def _head_rms(x, bd, g):
    sq = x * x
    hi, lo = _split_bf16(sq)
    ssum = jnp.dot(hi, bd, preferred_element_type=F32) + jnp.dot(lo, bd, preferred_element_type=F32)
    return x * lax.rsqrt(ssum * (1.0 / HEAD_DIM) + NORM_EPS) * g


def _inproj_kernel(x_ref, mod_ref, g1_ref, w_ref, cos_ref, sa_ref, sb_ref, bd_ref, gqg_ref, gkg_ref,
                   *out_refs, with_q, with_rope):
    x = x_ref[0]
    ms = jnp.mean(x * x, axis=-1, keepdims=True)
    y = x * lax.rsqrt(ms + NORM_EPS) * g1_ref[...]
    h = (y * (1.0 + mod_ref[0, 1:2, :]) + mod_ref[0, 0:1, :]).astype(BF16)
    p = jnp.dot(h, w_ref[...], preferred_element_type=F32)

    if with_rope:
        cos, sa, sb = cos_ref[...], sa_ref[...], sb_ref[...]
        rope = lambda t: _rope_tile(t, cos, sa, sb)
    else:
        rope = lambda t: t
    bd = bd_ref[...]
    scale = HEAD_DIM ** -0.5

    if with_q:
        dq_ref, gq_ref, dk_ref, gk_ref, dvt_ref, gvt_ref = out_refs
        off = 0
        for t in range(DIFF_WIDTH // LANES):
            sl = slice(t * LANES, (t + 1) * LANES)
            dq_ref[0, :, sl] = (rope(p[:, off + t * LANES: off + (t + 1) * LANES]) * scale).astype(BF16)
        off += DIFF_WIDTH
    else:
        dk_ref, gk_ref, dvt_ref, gvt_ref = out_refs
        off = 0
    for t in range(DIFF_WIDTH // LANES):
        sl = slice(t * LANES, (t + 1) * LANES)
        dk_ref[0, :, sl] = rope(p[:, off + t * LANES: off + (t + 1) * LANES]).astype(BF16)
    off += DIFF_WIDTH
    dvt_ref[0, 0] = p[:, off: off + DIFF_WIDTH].T.astype(BF16)
    off += DIFF_WIDTH
    if with_q:
        gqn = _head_rms(p[:, off: off + GQA_WIDTH], bd, gqg_ref[...])
        for t in range(GQA_WIDTH // LANES):
            sl = slice(t * LANES, (t + 1) * LANES)
            gq_ref[0, :, sl] = (rope(gqn[:, sl]) * scale).astype(BF16)
        off += GQA_WIDTH
    gkn = _head_rms(p[:, off: off + GQA_KV_WIDTH], bd[:GQA_KV_WIDTH, :GQA_KV_WIDTH], gkg_ref[...])
    gk_ref[0] = rope(gkn).astype(BF16)
    off += GQA_KV_WIDTH
    gvt_ref[0, 0] = p[:, off: off + GQA_KV_WIDTH].T.astype(BF16)


def _inproj_call(x, mod3, mod_row, g1, w, tables, bd, gqg, gkg, *, with_q, tile):
    b, s, d = x.shape
    n = w.shape[1]
    nt = s // tile
    cos, sa, sb = tables
    kv_shapes = [jax.ShapeDtypeStruct((b, s, DIFF_WIDTH), BF16),
                 jax.ShapeDtypeStruct((b, s, GQA_KV_WIDTH), BF16),
                 jax.ShapeDtypeStruct((b, nt, DIFF_WIDTH, tile), BF16),
                 jax.ShapeDtypeStruct((b, nt, GQA_KV_WIDTH, tile), BF16)]
    kv_specs = [pl.BlockSpec((1, tile, DIFF_WIDTH), lambda bi, i: (bi, i, 0)),
                pl.BlockSpec((1, tile, GQA_KV_WIDTH), lambda bi, i: (bi, i, 0)),
                pl.BlockSpec((1, 1, DIFF_WIDTH, tile), lambda bi, i: (bi, i, 0, 0)),
                pl.BlockSpec((1, 1, GQA_KV_WIDTH, tile), lambda bi, i: (bi, i, 0, 0))]
    if with_q:
        out_shapes = [jax.ShapeDtypeStruct((b, s, DIFF_WIDTH), BF16),
                      jax.ShapeDtypeStruct((b, s, GQA_WIDTH), BF16)] + kv_shapes
        out_specs = [pl.BlockSpec((1, tile, DIFF_WIDTH), lambda bi, i: (bi, i, 0)),
                     pl.BlockSpec((1, tile, GQA_WIDTH), lambda bi, i: (bi, i, 0))] + kv_specs
    else:
        out_shapes, out_specs = kv_shapes, kv_specs
    const = lambda bi, i: (0, 0)
    return pl.pallas_call(
        functools.partial(_inproj_kernel, with_q=with_q, with_rope=with_q),
        grid=(b, nt),
        in_specs=[pl.BlockSpec((1, tile, d), lambda bi, i: (bi, i, 0)),
                  pl.BlockSpec((1, 6, d), mod_row),
                  pl.BlockSpec((1, d), const),
                  pl.BlockSpec((d, n), const),
                  pl.BlockSpec((tile, LANES), lambda bi, i: (i, 0)),
                  pl.BlockSpec((tile, LANES), lambda bi, i: (i, 0)),
                  pl.BlockSpec((tile, LANES), lambda bi, i: (i, 0)),
                  pl.BlockSpec(bd.shape, const),
                  pl.BlockSpec(gqg.shape, const),
                  pl.BlockSpec(gkg.shape, const)],
        out_specs=out_specs,
        out_shape=out_shapes,
        compiler_params=_cparams(("parallel", "parallel")),
        name="inproj_x" if with_q else "inproj_ctx",
    )(x, mod3, g1, w, cos, sa, sb, bd, gqg, gkg)


def _attn_steps(ks, qs, vts, carries):
    ss = [lax.dot_general(k, q, (((1,), (1,)), ((), ())), preferred_element_type=F32) for k, q in zip(ks, qs)]
    stats = []
    for s, (m, l, _) in zip(ss, carries):
        m_new = jnp.maximum(m, jnp.max(s, axis=0, keepdims=True))
        alpha = jnp.exp(m - m_new)
        p = jnp.exp(s - m_new)
        stats.append((m_new, alpha, alpha * l + jnp.sum(p, axis=0, keepdims=True), p.astype(BF16)))
    return tuple((m_new, l, alpha * acc + jnp.dot(vt, p, preferred_element_type=F32))
                 for (m_new, alpha, l, p), vt, (_, _, acc) in zip(stats, vts, carries))


def _attn_kernel(dq_ref, gq_ref, dkx_ref, gkx_ref, dvx_ref, gvx_ref, dkc_ref, gkc_ref, dvc_ref, gvc_ref,
                 lamv_ref, subg_ref, o_ref):
    tq = dq_ref.shape[1]
    n_chunks, chunk = dvx_ref.shape[1], dvx_ref.shape[3]
    lv = lamv_ref[...]
    lam = (jnp.exp(jnp.sum(lv[0:1] * lv[1:2], axis=-1, keepdims=True))
           - jnp.exp(jnp.sum(lv[2:3] * lv[3:4], axis=-1, keepdims=True)) + LAMBDA_INIT)
    low = lax.broadcasted_iota(I32, (tq, LANES), 1) < HEAD_DIM

    def run_units(units):
        def go(kx_ref, kc_ref, vx_ref, vc_ref, dv):
            qs = [q for q, _, _ in units]
            init = tuple((jnp.full((1, tq), -jnp.inf, F32), jnp.zeros((1, tq), F32), jnp.zeros((dv, tq), F32))
                         for _ in units)
            carries = _attn_steps([kc_ref[0, :, col:col + LANES] for _, col, _ in units], qs,
                                  [vc_ref[0, 0, r0:r0 + dv, :] for _, _, r0 in units], init)

            def body(c, carries):
                off = pl.multiple_of(c * chunk, chunk)
                return _attn_steps([kx_ref[0, pl.ds(off, chunk), col:col + LANES] for _, col, _ in units], qs,
                                   [vx_ref[0, c, r0:r0 + dv, :] for _, _, r0 in units], carries)

            return [acc / l for _, l, acc in lax.fori_loop(0, n_chunks, body, carries)]
        return go

    zero = jnp.zeros((tq, LANES), BF16)
    for h0 in range(0, DIFF_HEADS, ATTN_GROUP // 2):
        heads = range(h0, h0 + ATTN_GROUP // 2)
        units = []
        for h in heads:
            qh = dq_ref[0, :, h * LANES:(h + 1) * LANES]
            units += [(jnp.where(low, qh, zero), h * LANES, h * DIFF_V_DIM),
                      (jnp.where(low, zero, qh), h * LANES, h * DIFF_V_DIM)]
        outs = run_units(units)(dkx_ref, dkc_ref, dvx_ref, dvc_ref, DIFF_V_DIM)
        for n, h in enumerate(heads):
            od = outs[2 * n] - lam * outs[2 * n + 1]
            ms = jnp.mean(od * od, axis=0, keepdims=True)
            od = od * lax.rsqrt(ms + NORM_EPS) * subg_ref[...] * (1.0 - LAMBDA_INIT)
            o_ref[0, :, h * LANES:(h + 1) * LANES] = od.T.astype(BF16)
    for t0 in range(0, GQA_Q_HEADS // 2, ATTN_GROUP // 2):
        tiles = range(t0, t0 + ATTN_GROUP // 2)
        units = []
        for t in tiles:
            qt = gq_ref[0, :, t * LANES:(t + 1) * LANES]
            units += [(jnp.where(low, qt, zero), 0, 0), (jnp.where(low, zero, qt), 0, HEAD_DIM)]
        outs = run_units(units)(gkx_ref, gkc_ref, gvx_ref, gvc_ref, HEAD_DIM)
        for n, t in enumerate(tiles):
            og = jnp.concatenate([outs[2 * n], outs[2 * n + 1]], axis=0)
            o_ref[0, :, DIFF_WIDTH + t * LANES: DIFF_WIDTH + (t + 1) * LANES] = og.T.astype(BF16)


def _attn_call(dq, gq, dkx, gkx, dvx, gvx, dkc, gkc, dvc, gvc, lamv, subg):
    b, s, _ = dq.shape
    nq = s // Q_TILE
    full3 = lambda a: pl.BlockSpec((1,) + a.shape[1:], lambda bi, i: (bi, 0, 0))
    full4 = lambda a: pl.BlockSpec((1,) + a.shape[1:], lambda bi, i: (bi, 0, 0, 0))
    const = lambda bi, i: (0, 0)
    return pl.pallas_call(
        _attn_kernel,
        grid=(b, nq),
        in_specs=[pl.BlockSpec((1, Q_TILE, DIFF_WIDTH), lambda bi, i: (bi, i, 0)),
                  pl.BlockSpec((1, Q_TILE, GQA_WIDTH), lambda bi, i: (bi, i, 0)),
                  full3(dkx), full3(gkx), full4(dvx), full4(gvx),
                  full3(dkc), full3(gkc), full4(dvc), full4(gvc),
                  pl.BlockSpec(lamv.shape, const), pl.BlockSpec(subg.shape, const)],
        out_specs=pl.BlockSpec((1, Q_TILE, DIFF_WIDTH + GQA_WIDTH), lambda bi, i: (bi, i, 0)),
        out_shape=jax.ShapeDtypeStruct((b, s, DIFF_WIDTH + GQA_WIDTH), BF16),
        compiler_params=_cparams(("parallel", "arbitrary")),
        name="attn",
    )(dq, gq, dkx, gkx, dvx, gvx, dkc, gkc, dvc, gvc, lamv, subg)


def _out_kernel(o_ref, w_ref, x_ref, mod_ref, g2_ref, x1_ref, h2_ref):
    attn = jnp.dot(o_ref[0], w_ref[...], preferred_element_type=F32)
    x1 = x_ref[0] + mod_ref[0, 2:3, :] * attn
    x1_ref[0] = x1
    ms = jnp.mean(x1 * x1, axis=-1, keepdims=True)
    y = x1 * lax.rsqrt(ms + NORM_EPS) * g2_ref[...]
    h2_ref[0] = (y * (1.0 + mod_ref[0, 4:5, :]) + mod_ref[0, 3:4, :]).astype(BF16)


def _out_call(o, w_out, x, mod3, g2):
    b, s, d = x.shape
    const = lambda bi, i: (0, 0)
    tile = lambda w: pl.BlockSpec((1, OUT_TILE, w), lambda bi, i: (bi, i, 0))
    return pl.pallas_call(
        _out_kernel,
        grid=(b, s // OUT_TILE),
        in_specs=[tile(o.shape[2]), pl.BlockSpec(w_out.shape, const), tile(d),
                  pl.BlockSpec((1, 6, d), lambda bi, i: (bi, 0, 0)), pl.BlockSpec((1, d), const)],
        out_specs=[tile(d), tile(d)],
        out_shape=[jax.ShapeDtypeStruct((b, s, d), F32), jax.ShapeDtypeStruct((b, s, d), BF16)],
        compiler_params=_cparams(("parallel", "parallel")),
        name="outproj",
    )(o, w_out, x, mod3, g2)


def _topk_sublanes(s, flat, k):
    t = s.shape[1]
    big = jnp.iinfo(jnp.int32).max
    rowk = lax.broadcasted_iota(I32, (k, t), 0)
    vals = jnp.zeros((k, t), F32)
    labs = jnp.zeros((k, t), I32)
    for r in range(k):
        m = jnp.max(s, axis=0, keepdims=True)
        lab = jnp.min(jnp.where(s == m, flat, big), axis=0, keepdims=True)
        vals = jnp.where(rowk == r, m, vals)
        labs = jnp.where(rowk == r, lab, labs)
        s = jnp.where(flat == lab, -jnp.inf, s)
    return vals, labs


def _select_head(h, qt_ref, sk_ref, gates_t_ref, idst_ref):
    ts = qt_ref.shape[1]
    k = PEER_TOPK
    key_iota = lax.broadcasted_iota(I32, (PEER_N_KEYS, ts), 0)
    sub8 = lax.broadcasted_iota(I32, (8, ts), 0)
    flat_rows = [lax.broadcasted_iota(I32, (k, ts), 0)] + [i * k + sub8 for i in range(1, k)]
    cand_flat = jnp.concatenate(flat_rows, axis=0)
    r0 = pl.multiple_of(h * 2 * PEER_HALF, 2 * PEER_HALF)
    tops = []
    for p in range(2):
        q = qt_ref[pl.ds(r0 + p * PEER_HALF, PEER_HALF), :]
        s = jnp.dot(sk_ref[2 * h + p], q, preferred_element_type=F32)
        tops.append(_topk_sublanes(s, key_iota, k))
    (v1, i1), (v2, i2) = tops
    vals = [v1[0:1] + v2] + [v1[i:i + 1] + v2[:8] for i in range(1, k)]
    eids = [i1[0:1] * PEER_N_KEYS + i2] + [i1[i:i + 1] * PEER_N_KEYS + i2[:8] for i in range(1, k)]
    cand = jnp.concatenate(vals, axis=0)
    cid = jnp.concatenate(eids, axis=0)
    best, pos = _topk_sublanes(cand, cand_flat, k)
    rowk = lax.broadcasted_iota(I32, (k, ts), 0)
    ids = jnp.zeros((k, ts), I32)
    for r in range(k):
        pick = jnp.sum(jnp.where(cand_flat == pos[r:r + 1], cid, 0), axis=0, keepdims=True)
        ids = jnp.where(rowk == r, pick, ids)
    e = jnp.exp(best - best[0:1])
    o0 = pl.multiple_of(h * k, k)
    gates_t_ref[pl.ds(o0, k), :] = e / jnp.sum(e, axis=0, keepdims=True)
    idst_ref[pl.ds(o0, k), :] = ids


def _peer_kernel(h_cur_ref, h_sel_ref, h_s0_ref, h_s1_ref, x1_ref, mod_ref, fg_ref, wqt_ref, sk_ref, uv_hbm, o_ref,
                 buf0, buf1, sem, ids_s, gates_s, qt_ref, idst_ref, idtok_ref, gates_t_ref, mix_ref, sem_s,
                 *, nsteps):
    tt = PEER_TILE
    nsel = PEER_HEADS * PEER_TOPK
    d = h_cur_ref.shape[1]
    npair = PEER_GROUP // 2
    i = pl.program_id(0)
    bufs = (buf0, buf1)
    r_cur, r_next, r_sel = lax.rem(i, 3), lax.rem(i + 1, 3), lax.rem(i + 2, 3)

    def row_copy(ring, tok, t, e, sl):
        return pltpu.make_async_copy(uv_hbm.at[ids_s[ring, tok, e]], bufs[sl].at[pl.ds(t * nsel + e, 1)],
                                     sem.at[sl])

    def issue(ring, tok0, sl):
        for t in range(tt):
            for e in range(nsel):
                row_copy(ring, tok0 + t, t, e, sl).start(priority=e % 2)

    def wait_slot(sl):
        pltpu.make_async_copy(bufs[1 - sl], bufs[sl], sem.at[sl]).wait()

    def mix(sl, k):
        buf = bufs[sl]
        t0 = pl.multiple_of(k * tt, tt)
        hx = h_cur_ref[pl.ds(t0, tt), :].astype(F32)
        lane_t = lax.broadcasted_iota(I32, (nsel, tt), 1)
        a = jnp.zeros((nsel, tt), F32)
        for t in range(tt):
            u = pltpu.bitcast(buf[t * nsel:(t + 1) * nsel, :] << 16, F32)
            prod = u * hx[t:t + 1, :]
            part = prod[:, 0:LANES]
            for c in range(1, d // LANES):
                part = part + prod[:, c * LANES:(c + 1) * LANES]
            a = jnp.where(lane_t == t, jnp.sum(part, axis=1, keepdims=True), a)
        gelu = 0.5 * a * (1.0 + lax.erf(a * (2.0 ** -0.5)))
        w = gates_s[r_cur, k] * gelu
        hi_mask = jnp.uint32(0xFFFF0000)
        for t in range(tt):
            v = pltpu.bitcast(buf[t * nsel:(t + 1) * nsel, :] & hi_mask, F32)
            mix_ref[pl.ds(t0 + t, 1), :] = jnp.sum(v * w[:, t:t + 1], axis=0, keepdims=True)

    def project(h_ref):
        qt_ref[...] = lax.dot_general(wqt_ref[...], h_ref[...], (((1,), (1,)), ((), ())),
                                      preferred_element_type=F32).astype(BF16)

    def publish(ring):
        idtok_ref[...] = idst_ref[...].T
        cp = pltpu.make_async_copy(idtok_ref, ids_s.at[ring], sem_s.at[0])
        cp.start()
        cp.wait()
        for k in range(PEER_GROUP):
            gates_s[ring, k] = gates_t_ref[:, k * tt:(k + 1) * tt]

    def select_all(h_ref, ring):
        project(h_ref)

        def head(h, c):
            _select_head(h, qt_ref, sk_ref, gates_t_ref, idst_ref)
            return c
        lax.fori_loop(0, PEER_HEADS, head, 0)
        publish(ring)

    @pl.when(i == 0)
    def _():
        select_all(h_s0_ref, 0)
        select_all(h_s1_ref, 1)

        def body(t, c):
            for e in range(nsel):
                row_copy(0, t, t, e, 0).start(priority=e % 2)
            return c
        lax.fori_loop(0, tt, body, 0)

    heads_per_pair = PEER_HEADS // npair

    def pair(p, c):
        k0 = 2 * p
        last = p == npair - 1
        wait_slot(0)
        issue(r_cur, (k0 + 1) * tt, 1)
        mix(0, k0)
        wait_slot(1)
        issue(jnp.where(last, r_next, r_cur), jnp.where(last, 0, (k0 + 2) * tt), 0)
        mix(1, k0 + 1)

        @pl.when(p == 0)
        def _():
            project(h_sel_ref)
        for hh in range(heads_per_pair):
            _select_head(p * heads_per_pair + hh, qt_ref, sk_ref, gates_t_ref, idst_ref)

        @pl.when(last)
        def _():
            publish(r_sel)
        return c

    lax.fori_loop(0, npair, pair, 0)

    y = x1_ref[...] + mod_ref[0, 5:6, :] * mix_ref[...]
    ms = jnp.mean(y * y, axis=-1, keepdims=True)
    o_ref[...] = y * lax.rsqrt(ms + NORM_EPS) * fg_ref[...]

    @pl.when(i == nsteps - 1)
    def _():
        wait_slot(0)


def _peer_call(h2, x1, mod3, fg, wqt, sk, uv, seq):
    t, d = h2.shape
    nsel = PEER_HEADS * PEER_TOPK
    tt = PEER_TILE
    rows_per_step = PEER_GROUP * tt
    assert rows_per_step == SEL_TILE and seq % rows_per_step == 0 and PEER_HEADS % (PEER_GROUP // 2) == 0
    nsteps = t // rows_per_step
    assert nsteps >= 2
    per_batch = seq // rows_per_step
    rows = lambda f: pl.BlockSpec((rows_per_step, d), f)
    return pl.pallas_call(
        functools.partial(_peer_kernel, nsteps=nsteps),
        grid=(nsteps,),
        in_specs=[rows(lambda i: (i, 0)),
                  rows(lambda i: (jnp.minimum(i + 2, nsteps - 1), 0)),
                  rows(lambda i: (0, 0)),
                  rows(lambda i: (1, 0)),
                  rows(lambda i: (i, 0)),
                  pl.BlockSpec((1, 6, d), lambda i: (i // per_batch, 0, 0)),
                  pl.BlockSpec((1, d), lambda i: (0, 0)),
                  pl.BlockSpec(wqt.shape, lambda i: (0, 0)),
                  pl.BlockSpec(sk.shape, lambda i: (0, 0, 0)),
                  pl.BlockSpec(memory_space=pl.ANY)],
        out_specs=rows(lambda i: (i, 0)),
        out_shape=jax.ShapeDtypeStruct((t, d), F32),
        scratch_shapes=[pltpu.VMEM((tt * nsel, d), jnp.uint32),
                        pltpu.VMEM((tt * nsel, d), jnp.uint32),
                        pltpu.SemaphoreType.DMA((2,)),
                        pltpu.SMEM((3, rows_per_step, nsel), I32),
                        pltpu.VMEM((3, PEER_GROUP, nsel, tt), F32),
                        pltpu.VMEM((wqt.shape[0], rows_per_step), BF16),
                        pltpu.VMEM((nsel, rows_per_step), I32),
                        pltpu.VMEM((rows_per_step, nsel), I32),
                        pltpu.VMEM((nsel, rows_per_step), F32),
                        pltpu.VMEM((rows_per_step, d), F32),
                        pltpu.SemaphoreType.DMA((1,))],
        compiler_params=_cparams(("arbitrary",)),
        name="peer",
    )(h2, h2, h2, h2, x1, mod3, fg, wqt, sk, uv)


def _rope_tables(seq):
    half = HEAD_DIM // 4
    freqs = ROPE_THETA ** (-jnp.arange(half, dtype=F32) / half)
    pos = jnp.arange(seq, dtype=jnp.int32)
    ang_r = (pos // GRID_W).astype(F32)[:, None] * freqs[None, :]
    ang_c = (pos % GRID_W).astype(F32)[:, None] * freqs[None, :]
    z = jnp.zeros_like(ang_r)
    cos = jnp.concatenate([jnp.cos(ang_r)] * 2 + [jnp.cos(ang_c)] * 2, axis=1)
    sa = jnp.concatenate([-jnp.sin(ang_r), z, -jnp.sin(ang_c), z], axis=1)
    sb = jnp.concatenate([z, jnp.sin(ang_r), z, jnp.sin(ang_c)], axis=1)
    rep = LANES // HEAD_DIM
    return tuple(jnp.tile(a, (1, rep)) for a in (cos, sa, sb))


def kernel(x, c, ctx, c_ctx, w_mod, b_mod, norm1_g, norm2_g, w_in, w_out, diff_lq1, diff_lk1, diff_lq2,
           diff_lk2, diff_subln_g, gqa_q_norm_g, gqa_k_norm_g, peer_wq, peer_subkeys, peer_u, peer_v,
           final_norm_g):
    b, s, d = x.shape
    assert w_mod.shape[0] == 1, "depth-1 block"
    assert s % IN_TILE == 0 and s % Q_TILE == 0 and ctx.shape[1] % LANES == 0

    pad = (-(b + 1)) % 8
    cc = jnp.concatenate([c, c_ctx[None, :], jnp.zeros((pad, d), F32)], axis=0)
    mod3 = _mod_call(cc, w_mod[0], b_mod[0][None, :]).reshape(cc.shape[0], 6, d)

    w0 = w_in[0]
    o_dk, o_dv, o_gq, o_gk, o_gv = (DIFF_WIDTH, 2 * DIFF_WIDTH, 3 * DIFF_WIDTH, 3 * DIFF_WIDTH + GQA_WIDTH,
                                    3 * DIFF_WIDTH + GQA_WIDTH + GQA_KV_WIDTH)
    group = GQA_Q_HEADS // GQA_KV_HEADS
    head_order = [j * group + t for t in range(group) for j in range(GQA_KV_HEADS)]
    gq_cols = jnp.asarray([hd * HEAD_DIM + e for hd in head_order for e in range(HEAD_DIM)], dtype=jnp.int32)
    w_gq = w0[:, o_gq:o_gk][:, gq_cols]
    w_x = jnp.concatenate([w0[:, :o_gq], w_gq, w0[:, o_gk:]], axis=1).astype(BF16)
    w_c = jnp.concatenate([w0[:, o_dk:o_gq], w0[:, o_gk:]], axis=1).astype(BF16)
    w_o = jnp.concatenate([w_out[0][:DIFF_WIDTH], w_out[0][DIFF_WIDTH:][gq_cols]], axis=0).astype(BF16)

    seg = jnp.arange(GQA_WIDTH) // HEAD_DIM
    bd = (seg[:, None] == seg[None, :]).astype(BF16)
    gqg = jnp.tile(gqa_q_norm_g[0], GQA_Q_HEADS)[None, :]
    gkg = jnp.tile(gqa_k_norm_g[0], GQA_KV_HEADS)[None, :]
    g1 = norm1_g[0][None, :]
    tables = _rope_tables(s)

    dq, gq, dkx, gkx, dvx, gvx = _inproj_call(x, mod3, lambda bi, i: (bi, 0, 0), g1, w_x, tables, bd, gqg, gkg,
                                              with_q=True, tile=IN_TILE)
    ctx_tile = ctx.shape[1]
    ctx_tables = tuple(a[:ctx_tile] for a in tables)
    dkc, gkc, dvc, gvc = _inproj_call(ctx, mod3, lambda bi, i: (b, 0, 0), g1, w_c, ctx_tables, bd, gqg, gkg,
                                      with_q=False, tile=ctx_tile)

    lamv = jnp.stack([diff_lq1[0], diff_lk1[0], diff_lq2[0], diff_lk2[0]], axis=0).astype(F32)
    subg = diff_subln_g[0][:, None]
    o = _attn_call(dq, gq, dkx, gkx, dvx, gvx, dkc, gkc, dvc, gvc, lamv, subg)

    x1, h2 = _out_call(o, w_o, x, mod3, norm2_g[0][None, :])

    t = b * s
    h2f = h2.reshape(t, d)
    wqt = peer_wq[0].T.astype(BF16)
    sk = peer_subkeys[0].reshape(PEER_HEADS * 2, PEER_N_KEYS, PEER_HALF).astype(BF16)
    half_bits = lambda a: lax.bitcast_convert_type(a.astype(BF16), jnp.uint16).astype(jnp.uint32)
    uv = (half_bits(peer_u[0]) | (half_bits(peer_v[0]) << 16))[:, None, :]
    out = _peer_call(h2f, x1.reshape(t, d), mod3, final_norm_g[None, :], wqt, sk, uv, s)
    return out.reshape(b, s, d)
```

```python
import functools
import math

import jax
import jax.numpy as jnp
from jax import lax
from jax.experimental import pallas as pl
from jax.experimental.pallas import tpu as pltpu

F32 = jnp.float32
BF16 = jnp.bfloat16
I32 = jnp.int32

HEAD_DIM = 64
GRID_W = 64
ROPE_THETA = 10000.0
NORM_EPS = 1e-6
DIFF_HEADS = 4
DIFF_V_DIM = 2 * HEAD_DIM
DIFF_WIDTH = DIFF_HEADS * DIFF_V_DIM
GQA_Q_HEADS = 8
GQA_KV_HEADS = 2
GQA_WIDTH = GQA_Q_HEADS * HEAD_DIM
GQA_KV_WIDTH = GQA_KV_HEADS * HEAD_DIM
PEER_HEADS = 8
PEER_N_KEYS = 128
PEER_HALF = 128
PEER_TOPK = 16
LAMBDA_INIT = 0.8 - 0.6 * math.exp(-0.3 * 0)
LANES = 128
VMEM_LIMIT = 56 * 1024 * 1024

IN_TILE = 512
Q_TILE = 256
OUT_TILE = 256
SEL_TILE = 128
PEER_TILE = 16
PEER_GROUP = SEL_TILE // PEER_TILE
ATTN_GROUP = 8


def _cparams(sem):
    return pltpu.CompilerParams(dimension_semantics=sem, vmem_limit_bytes=VMEM_LIMIT)


def _split_bf16(a):
    hi = a.astype(BF16)
    lo = (a - hi.astype(F32)).astype(BF16)
    return hi, lo


def _mod_kernel(c_ref, w_ref, b_ref, o_ref):
    s = jax.nn.silu(c_ref[...])
    s_hi, s_lo = _split_bf16(s)
    w_hi, w_lo = _split_bf16(w_ref[...])
    acc = jnp.dot(s_hi, w_hi, preferred_element_type=F32)
    acc += jnp.dot(s_hi, w_lo, preferred_element_type=F32)
    acc += jnp.dot(s_lo, w_hi, preferred_element_type=F32)
    o_ref[...] = acc + b_ref[...]


def _mod_call(cc, w_mod, b_mod):
    rows, d = cc.shape
    n = w_mod.shape[1]
    tn = n // 4
    return pl.pallas_call(
        _mod_kernel,
        grid=(n // tn,),
        in_specs=[pl.BlockSpec((rows, d), lambda j: (0, 0)),
                  pl.BlockSpec((d, tn), lambda j: (0, j)),
                  pl.BlockSpec((1, tn), lambda j: (0, j))],
        out_specs=pl.BlockSpec((rows, tn), lambda j: (0, j)),
        out_shape=jax.ShapeDtypeStruct((rows, n), F32),
        compiler_params=_cparams(("arbitrary",)),
        name="mod",
    )(cc, w_mod, b_mod)


def _rope_tile(x, cos, sa, sb):
    return x * cos + pltpu.roll(x, LANES - 16, 1) * sa + pltpu.roll(x, 16, 1) * sb


def _head_rms(x, bd, g):
    sq = x * x
    hi, lo = _split_bf16(sq)
    ssum = jnp.dot(hi, bd, preferred_element_type=F32) + jnp.dot(lo, bd, preferred_element_type=F32)
    return x * lax.rsqrt(ssum * (1.0 / HEAD_DIM) + NORM_EPS) * g


def _inproj_kernel(x_ref, mod_ref, g1_ref, w_ref, cos_ref, sa_ref, sb_ref, bd_ref, gqg_ref, gkg_ref,
                   *out_refs, with_q, with_rope):
    x = x_ref[0]
    ms = jnp.mean(x * x, axis=-1, keepdims=True)
    y = x * lax.rsqrt(ms + NORM_EPS) * g1_ref[...]
    h = (y * (1.0 + mod_ref[0, 1:2, :]) + mod_ref[0, 0:1, :]).astype(BF16)
    p = jnp.dot(h, w_ref[...], preferred_element_type=F32)

    if with_rope:
        cos, sa, sb = cos_ref[...], sa_ref[...], sb_ref[...]
        rope = lambda t: _rope_tile(t, cos, sa, sb)
    else:
        rope = lambda t: t
    bd = bd_ref[...]
    scale = HEAD_DIM ** -0.5

    if with_q:
        dq_ref, gq_ref, dk_ref, gk_ref, dvt_ref, gvt_ref = out_refs
        off = 0
        for t in range(DIFF_WIDTH // LANES):
            sl = slice(t * LANES, (t + 1) * LANES)
            dq_ref[0, :, sl] = (rope(p[:, off + t * LANES: off + (t + 1) * LANES]) * scale).astype(BF16)
        off += DIFF_WIDTH
    else:
        dk_ref, gk_ref, dvt_ref, gvt_ref = out_refs
        off = 0
    for t in range(DIFF_WIDTH // LANES):
        sl = slice(t * LANES, (t + 1) * LANES)
        dk_ref[0, :, sl] = rope(p[:, off + t * LANES: off + (t + 1) * LANES]).astype(BF16)
    off += DIFF_WIDTH
    dvt_ref[0, 0] = p[:, off: off + DIFF_WIDTH].T.astype(BF16)
    off += DIFF_WIDTH
    if with_q:
        gqn = _head_rms(p[:, off: off + GQA_WIDTH], bd, gqg_ref[...])
        for t in range(GQA_WIDTH // LANES):
            sl = slice(t * LANES, (t + 1) * LANES)
            gq_ref[0, :, sl] = (rope(gqn[:, sl]) * scale).astype(BF16)
        off += GQA_WIDTH
    gkn = _head_rms(p[:, off: off + GQA_KV_WIDTH], bd[:GQA_KV_WIDTH, :GQA_KV_WIDTH], gkg_ref[...])
    gk_ref[0] = rope(gkn).astype(BF16)
    off += GQA_KV_WIDTH
    gvt_ref[0, 0] = p[:, off: off + GQA_KV_WIDTH].T.astype(BF16)


def _inproj_call(x, mod3, mod_row, g1, w, tables, bd, gqg, gkg, *, with_q, tile):
    b, s, d = x.shape
    n = w.shape[1]
    nt = s // tile
    cos, sa, sb = tables
    kv_shapes = [jax.ShapeDtypeStruct((b, s, DIFF_WIDTH), BF16),
                 jax.ShapeDtypeStruct((b, s, GQA_KV_WIDTH), BF16),
                 jax.ShapeDtypeStruct((b, nt, DIFF_WIDTH, tile), BF16),
                 jax.ShapeDtypeStruct((b, nt, GQA_KV_WIDTH, tile), BF16)]
    kv_specs = [pl.BlockSpec((1, tile, DIFF_WIDTH), lambda bi, i: (bi, i, 0)),
                pl.BlockSpec((1, tile, GQA_KV_WIDTH), lambda bi, i: (bi, i, 0)),
                pl.BlockSpec((1, 1, DIFF_WIDTH, tile), lambda bi, i: (bi, i, 0, 0)),
                pl.BlockSpec((1, 1, GQA_KV_WIDTH, tile), lambda bi, i: (bi, i, 0, 0))]
    if with_q:
        out_shapes = [jax.ShapeDtypeStruct((b, s, DIFF_WIDTH), BF16),
                      jax.ShapeDtypeStruct((b, s, GQA_WIDTH), BF16)] + kv_shapes
        out_specs = [pl.BlockSpec((1, tile, DIFF_WIDTH), lambda bi, i: (bi, i, 0)),
                     pl.BlockSpec((1, tile, GQA_WIDTH), lambda bi, i: (bi, i, 0))] + kv_specs
    else:
        out_shapes, out_specs = kv_shapes, kv_specs
    const = lambda bi, i: (0, 0)
    return pl.pallas_call(
        functools.partial(_inproj_kernel, with_q=with_q, with_rope=with_q),
        grid=(b, nt),
        in_specs=[pl.BlockSpec((1, tile, d), lambda bi, i: (bi, i, 0)),
                  pl.BlockSpec((1, 6, d), mod_row),
                  pl.BlockSpec((1, d), const),
                  pl.BlockSpec((d, n), const),
                  pl.BlockSpec((tile, LANES), lambda bi, i: (i, 0)),
                  pl.BlockSpec((tile, LANES), lambda bi, i: (i, 0)),
                  pl.BlockSpec((tile, LANES), lambda bi, i: (i, 0)),
                  pl.BlockSpec(bd.shape, const),
                  pl.BlockSpec(gqg.shape, const),
                  pl.BlockSpec(gkg.shape, const)],
        out_specs=out_specs,
        out_shape=out_shapes,
        compiler_params=_cparams(("parallel", "parallel")),
        name="inproj_x" if with_q else "inproj_ctx",
    )(x, mod3, g1, w, cos, sa, sb, bd, gqg, gkg)


def _attn_steps(ks, qs, vts, carries):
    ss = [lax.dot_general(k, q, (((1,), (1,)), ((), ())), preferred_element_type=F32) for k, q in zip(ks, qs)]
    stats = []
    for s, (m, l, _) in zip(ss, carries):
        m_new = jnp.maximum(m, jnp.max(s, axis=0, keepdims=True))
        alpha = jnp.exp(m - m_new)
        p = jnp.exp(s - m_new)
        stats.append((m_new, alpha, alpha * l + jnp.sum(p, axis=0, keepdims=True), p.astype(BF16)))
    return tuple((m_new, l, alpha * acc + jnp.dot(vt, p, preferred_element_type=F32))
                 for (m_new, alpha, l, p), vt, (_, _, acc) in zip(stats, vts, carries))


def _attn_kernel(dq_ref, gq_ref, dkx_ref, gkx_ref, dvx_ref, gvx_ref, dkc_ref, gkc_ref, dvc_ref, gvc_ref,
                 lamv_ref, subg_ref, o_ref):
    tq = dq_ref.shape[1]
    n_chunks, chunk = dvx_ref.shape[1], dvx_ref.shape[3]
    lv = lamv_ref[...]
    lam = (jnp.exp(jnp.sum(lv[0:1] * lv[1:2], axis=-1, keepdims=True))
           - jnp.exp(jnp.sum(lv[2:3] * lv[3:4], axis=-1, keepdims=True)) + LAMBDA_INIT)
    low = lax.broadcasted_iota(I32, (tq, LANES), 1) < HEAD_DIM

    def run_units(units):
        def go(kx_ref, kc_ref, vx_ref, vc_ref, dv):
            qs = [q for q, _, _ in units]
            init = tuple((jnp.full((1, tq), -jnp.inf, F32), jnp.zeros((1, tq), F32), jnp.zeros((dv, tq), F32))
                         for _ in units)
            carries = _attn_steps([kc_ref[0, :, col:col + LANES] for _, col, _ in units], qs,
                                  [vc_ref[0, 0, r0:r0 + dv, :] for _, _, r0 in units], init)

            def body(c, carries):
                off = pl.multiple_of(c * chunk, chunk)
                return _attn_steps([kx_ref[0, pl.ds(off, chunk), col:col + LANES] for _, col, _ in units], qs,
                                   [vx_ref[0, c, r0:r0 + dv, :] for _, _, r0 in units], carries)

            return [acc / l for _, l, acc in lax.fori_loop(0, n_chunks, body, carries)]
        return go

    zero = jnp.zeros((tq, LANES), BF16)
    for h0 in range(0, DIFF_HEADS, ATTN_GROUP // 2):
        heads = range(h0, h0 + ATTN_GROUP // 2)
        units = []
        for h in heads:
            qh = dq_ref[0, :, h * LANES:(h + 1) * LANES]
            units += [(jnp.where(low, qh, zero), h * LANES, h * DIFF_V_DIM),
                      (jnp.where(low, zero, qh), h * LANES, h * DIFF_V_DIM)]
        outs = run_units(units)(dkx_ref, dkc_ref, dvx_ref, dvc_ref, DIFF_V_DIM)
        for n, h in enumerate(heads):
            od = outs[2 * n] - lam * outs[2 * n + 1]
            ms = jnp.mean(od * od, axis=0, keepdims=True)
            od = od * lax.rsqrt(ms + NORM_EPS) * subg_ref[...] * (1.0 - LAMBDA_INIT)
            o_ref[0, :, h * LANES:(h + 1) * LANES] = od.T.astype(BF16)
    for t0 in range(0, GQA_Q_HEADS // 2, ATTN_GROUP // 2):
        tiles = range(t0, t0 + ATTN_GROUP // 2)
        units = []
        for t in tiles:
            qt = gq_ref[0, :, t * LANES:(t + 1) * LANES]
            units += [(jnp.where(low, qt, zero), 0, 0), (jnp.where(low, zero, qt), 0, HEAD_DIM)]
        outs = run_units(units)(gkx_ref, gkc_ref, gvx_ref, gvc_ref, HEAD_DIM)
        for n, t in enumerate(tiles):
            og = jnp.concatenate([outs[2 * n], outs[2 * n + 1]], axis=0)
            o_ref[0, :, DIFF_WIDTH + t * LANES: DIFF_WIDTH + (t + 1) * LANES] = og.T.astype(BF16)


def _attn_call(dq, gq, dkx, gkx, dvx, gvx, dkc, gkc, dvc, gvc, lamv, subg):
    b, s, _ = dq.shape
    nq = s // Q_TILE
    full3 = lambda a: pl.BlockSpec((1,) + a.shape[1:], lambda bi, i: (bi, 0, 0))
    full4 = lambda a: pl.BlockSpec((1,) + a.shape[1:], lambda bi, i: (bi, 0, 0, 0))
    const = lambda bi, i: (0, 0)
    return pl.pallas_call(
        _attn_kernel,
        grid=(b, nq),
        in_specs=[pl.BlockSpec((1, Q_TILE, DIFF_WIDTH), lambda bi, i: (bi, i, 0)),
                  pl.BlockSpec((1, Q_TILE, GQA_WIDTH), lambda bi, i: (bi, i, 0)),
                  full3(dkx), full3(gkx), full4(dvx), full4(gvx),
                  full3(dkc), full3(gkc), full4(dvc), full4(gvc),
                  pl.BlockSpec(lamv.shape, const), pl.BlockSpec(subg.shape, const)],
        out_specs=pl.BlockSpec((1, Q_TILE, DIFF_WIDTH + GQA_WIDTH), lambda bi, i: (bi, i, 0)),
        out_shape=jax.ShapeDtypeStruct((b, s, DIFF_WIDTH + GQA_WIDTH), BF16),
        compiler_params=_cparams(("parallel", "arbitrary")),
        name="attn",
    )(dq, gq, dkx, gkx, dvx, gvx, dkc, gkc, dvc, gvc, lamv, subg)


def _out_kernel(o_ref, w_ref, x_ref, mod_ref, g2_ref, x1_ref, h2_ref):
    attn = jnp.dot(o_ref[0], w_ref[...], preferred_element_type=F32)
    x1 = x_ref[0] + mod_ref[0, 2:3, :] * attn
    x1_ref[0] = x1
    ms = jnp.mean(x1 * x1, axis=-1, keepdims=True)
    y = x1 * lax.rsqrt(ms + NORM_EPS) * g2_ref[...]
    h2_ref[0] = (y * (1.0 + mod_ref[0, 4:5, :]) + mod_ref[0, 3:4, :]).astype(BF16)


def _out_call(o, w_out, x, mod3, g2):
    b, s, d = x.shape
    const = lambda bi, i: (0, 0)
    tile = lambda w: pl.BlockSpec((1, OUT_TILE, w), lambda bi, i: (bi, i, 0))
    return pl.pallas_call(
        _out_kernel,
        grid=(b, s // OUT_TILE),
        in_specs=[tile(o.shape[2]), pl.BlockSpec(w_out.shape, const), tile(d),
                  pl.BlockSpec((1, 6, d), lambda bi, i: (bi, 0, 0)), pl.BlockSpec((1, d), const)],
        out_specs=[tile(d), tile(d)],
        out_shape=[jax.ShapeDtypeStruct((b, s, d), F32), jax.ShapeDtypeStruct((b, s, d), BF16)],
        compiler_params=_cparams(("parallel", "parallel")),
        name="outproj",
    )(o, w_out, x, mod3, g2)


def _topk_sublanes(s, flat, k):
    t = s.shape[1]
    big = jnp.iinfo(jnp.int32).max
    rowk = lax.broadcasted_iota(I32, (k, t), 0)
    vals = jnp.zeros((k, t), F32)
    labs = jnp.zeros((k, t), I32)
    for r in range(k):
        m = jnp.max(s, axis=0, keepdims=True)
        lab = jnp.min(jnp.where(s == m, flat, big), axis=0, keepdims=True)
        vals = jnp.where(rowk == r, m, vals)
        labs = jnp.where(rowk == r, lab, labs)
        s = jnp.where(flat == lab, -jnp.inf, s)
    return vals, labs


def _select_head(h, qt_ref, sk_ref, gates_t_ref, idst_ref):
    ts = qt_ref.shape[1]
    k = PEER_TOPK
    key_iota = lax.broadcasted_iota(I32, (PEER_N_KEYS, ts), 0)
    sub8 = lax.broadcasted_iota(I32, (8, ts), 0)
    flat_rows = [lax.broadcasted_iota(I32, (k, ts), 0)] + [i * k + sub8 for i in range(1, k)]
    cand_flat = jnp.concatenate(flat_rows, axis=0)
    r0 = pl.multiple_of(h * 2 * PEER_HALF, 2 * PEER_HALF)
    tops = []
    for p in range(2):
        q = qt_ref[pl.ds(r0 + p * PEER_HALF, PEER_HALF), :]
        s = jnp.dot(sk_ref[2 * h + p], q, preferred_element_type=F32)
        tops.append(_topk_sublanes(s, key_iota, k))
    (v1, i1), (v2, i2) = tops
    vals = [v1[0:1] + v2] + [v1[i:i + 1] + v2[:8] for i in range(1, k)]
    eids = [i1[0:1] * PEER_N_KEYS + i2] + [i1[i:i + 1] * PEER_N_KEYS + i2[:8] for i in range(1, k)]
    cand = jnp.concatenate(vals, axis=0)
    cid = jnp.concatenate(eids, axis=0)
    best, pos = _topk_sublanes(cand, cand_flat, k)
    rowk = lax.broadcasted_iota(I32, (k, ts), 0)
    ids = jnp.zeros((k, ts), I32)
    for r in range(k):
        pick = jnp.sum(jnp.where(cand_flat == pos[r:r + 1], cid, 0), axis=0, keepdims=True)
        ids = jnp.where(rowk == r, pick, ids)
    e = jnp.exp(best - best[0:1])
    o0 = pl.multiple_of(h * k, k)
    gates_t_ref[pl.ds(o0, k), :] = e / jnp.sum(e, axis=0, keepdims=True)
    idst_ref[pl.ds(o0, k), :] = ids


def _peer_kernel(h_cur_ref, h_sel_ref, h_s0_ref, h_s1_ref, x1_ref, mod_ref, fg_ref, wqt_ref, sk_ref, uv_hbm, o_ref,
                 buf0, buf1, sem, ids_s, gates_s, qt_ref, idst_ref, idtok_ref, gates_t_ref, mix_ref, sem_s,
                 *, nsteps):
    tt = PEER_TILE
    nsel = PEER_HEADS * PEER_TOPK
    d = h_cur_ref.shape[1]
    npair = PEER_GROUP // 2
    i = pl.program_id(0)
    bufs = (buf0, buf1)
    r_cur, r_next, r_sel = lax.rem(i, 3), lax.rem(i + 1, 3), lax.rem(i + 2, 3)

    def row_copy(ring, tok, t, e, sl):
        return pltpu.make_async_copy(uv_hbm.at[ids_s[ring, tok, e]], bufs[sl].at[pl.ds(t * nsel + e, 1)],
                                     sem.at[sl])

    def issue(ring, tok0, sl):
        for t in range(tt):
            for e in range(nsel):
                row_copy(ring, tok0 + t, t, e, sl).start(priority=e % 2)

    def wait_slot(sl):
        pltpu.make_async_copy(bufs[1 - sl], bufs[sl], sem.at[sl]).wait()

    def mix(sl, k):
        buf = bufs[sl]
        t0 = pl.multiple_of(k * tt, tt)
        hx = h_cur_ref[pl.ds(t0, tt), :].astype(F32)
        lane_t = lax.broadcasted_iota(I32, (nsel, tt), 1)
        a = jnp.zeros((nsel, tt), F32)
        for t in range(tt):
            u = pltpu.bitcast(buf[t * nsel:(t + 1) * nsel, :] << 16, F32)
            prod = u * hx[t:t + 1, :]
            part = prod[:, 0:LANES]
            for c in range(1, d // LANES):
                part = part + prod[:, c * LANES:(c + 1) * LANES]
            a = jnp.where(lane_t == t, jnp.sum(part, axis=1, keepdims=True), a)
        gelu = 0.5 * a * (1.0 + lax.erf(a * (2.0 ** -0.5)))
        w = gates_s[r_cur, k] * gelu
        hi_mask = jnp.uint32(0xFFFF0000)
        for t in range(tt):
            v = pltpu.bitcast(buf[t * nsel:(t + 1) * nsel, :] & hi_mask, F32)
            mix_ref[pl.ds(t0 + t, 1), :] = jnp.sum(v * w[:, t:t + 1], axis=0, keepdims=True)

    def project(h_ref):
        qt_ref[...] = lax.dot_general(wqt_ref[...], h_ref[...], (((1,), (1,)), ((), ())),
                                      preferred_element_type=F32).astype(BF16)

    def publish(ring):
        idtok_ref[...] = idst_ref[...].T
        cp = pltpu.make_async_copy(idtok_ref, ids_s.at[ring], sem_s.at[0])
        cp.start()
        cp.wait()
        for k in range(PEER_GROUP):
            gates_s[ring, k] = gates_t_ref[:, k * tt:(k + 1) * tt]

    def select_all(h_ref, ring):
        project(h_ref)

        def head(h, c):
            _select_head(h, qt_ref, sk_ref, gates_t_ref, idst_ref)
            return c
        lax.fori_loop(0, PEER_HEADS, head, 0)
        publish(ring)

    @pl.when(i == 0)
    def _():
        select_all(h_s0_ref, 0)
        select_all(h_s1_ref, 1)

        def body(t, c):
            for e in range(nsel):
                row_copy(0, t, t, e, 0).start(priority=e % 2)
            return c
        lax.fori_loop(0, tt, body, 0)

    assert PEER_HEADS == 2 * npair

    def pair(p, c):
        k0 = 2 * p
        last = p == npair - 1

        @pl.when(p == 0)
        def _():
            project(h_sel_ref)

        wait_slot(0)
        issue(r_cur, (k0 + 1) * tt, 1)
        mix(0, k0)
        _select_head(2 * p, qt_ref, sk_ref, gates_t_ref, idst_ref)
        wait_slot(1)
        issue(jnp.where(last, r_next, r_cur), jnp.where(last, 0, (k0 + 2) * tt), 0)
        mix(1, k0 + 1)
        _select_head(2 * p + 1, qt_ref, sk_ref, gates_t_ref, idst_ref)

        @pl.when(last)
        def _():
            publish(r_sel)
        return c

    lax.fori_loop(0, npair, pair, 0)

    y = x1_ref[...] + mod_ref[0, 5:6, :] * mix_ref[...]
    ms = jnp.mean(y * y, axis=-1, keepdims=True)
    o_ref[...] = y * lax.rsqrt(ms + NORM_EPS) * fg_ref[...]

    @pl.when(i == nsteps - 1)
    def _():
        wait_slot(0)


def _peer_call(h2, x1, mod3, fg, wqt, sk, uv, seq):
    t, d = h2.shape
    nsel = PEER_HEADS * PEER_TOPK
    tt = PEER_TILE
    rows_per_step = PEER_GROUP * tt
    assert rows_per_step == SEL_TILE and seq % rows_per_step == 0 and PEER_HEADS % (PEER_GROUP // 2) == 0
    nsteps = t // rows_per_step
    assert nsteps >= 2
    per_batch = seq // rows_per_step
    rows = lambda f: pl.BlockSpec((rows_per_step, d), f)
    return pl.pallas_call(
        functools.partial(_peer_kernel, nsteps=nsteps),
        grid=(nsteps,),
        in_specs=[rows(lambda i: (i, 0)),
                  rows(lambda i: (jnp.minimum(i + 2, nsteps - 1), 0)),
                  rows(lambda i: (0, 0)),
                  rows(lambda i: (1, 0)),
                  rows(lambda i: (i, 0)),
                  pl.BlockSpec((1, 6, d), lambda i: (i // per_batch, 0, 0)),
                  pl.BlockSpec((1, d), lambda i: (0, 0)),
                  pl.BlockSpec(wqt.shape, lambda i: (0, 0)),
                  pl.BlockSpec(sk.shape, lambda i: (0, 0, 0)),
                  pl.BlockSpec(memory_space=pl.ANY)],
        out_specs=rows(lambda i: (i, 0)),
        out_shape=jax.ShapeDtypeStruct((t, d), F32),
        scratch_shapes=[pltpu.VMEM((tt * nsel, d), jnp.uint32),
                        pltpu.VMEM((tt * nsel, d), jnp.uint32),
                        pltpu.SemaphoreType.DMA((2,)),
                        pltpu.SMEM((3, rows_per_step, nsel), I32),
                        pltpu.VMEM((3, PEER_GROUP, nsel, tt), F32),
                        pltpu.VMEM((wqt.shape[0], rows_per_step), BF16),
                        pltpu.VMEM((nsel, rows_per_step), I32),
                        pltpu.VMEM((rows_per_step, nsel), I32),
                        pltpu.VMEM((nsel, rows_per_step), F32),
                        pltpu.VMEM((rows_per_step, d), F32),
                        pltpu.SemaphoreType.DMA((1,))],
        compiler_params=_cparams(("arbitrary",)),
        name="peer",
    )(h2, h2, h2, h2, x1, mod3, fg, wqt, sk, uv)


def _rope_tables(seq):
    half = HEAD_DIM // 4
    freqs = ROPE_THETA ** (-jnp.arange(half, dtype=F32) / half)
    pos = jnp.arange(seq, dtype=jnp.int32)
    ang_r = (pos // GRID_W).astype(F32)[:, None] * freqs[None, :]
    ang_c = (pos % GRID_W).astype(F32)[:, None] * freqs[None, :]
    z = jnp.zeros_like(ang_r)
    cos = jnp.concatenate([jnp.cos(ang_r)] * 2 + [jnp.cos(ang_c)] * 2, axis=1)
    sa = jnp.concatenate([-jnp.sin(ang_r), z, -jnp.sin(ang_c), z], axis=1)
    sb = jnp.concatenate([z, jnp.sin(ang_r), z, jnp.sin(ang_c)], axis=1)
    rep = LANES // HEAD_DIM
    return tuple(jnp.tile(a, (1, rep)) for a in (cos, sa, sb))


def kernel(x, c, ctx, c_ctx, w_mod, b_mod, norm1_g, norm2_g, w_in, w_out, diff_lq1, diff_lk1, diff_lq2,
           diff_lk2, diff_subln_g, gqa_q_norm_g, gqa_k_norm_g, peer_wq, peer_subkeys, peer_u, peer_v,
           final_norm_g):
    b, s, d = x.shape
    assert w_mod.shape[0] == 1, "depth-1 block"
    assert s % IN_TILE == 0 and s % Q_TILE == 0 and ctx.shape[1] % LANES == 0

    pad = (-(b + 1)) % 8
    cc = jnp.concatenate([c, c_ctx[None, :], jnp.zeros((pad, d), F32)], axis=0)
    mod3 = _mod_call(cc, w_mod[0], b_mod[0][None, :]).reshape(cc.shape[0], 6, d)

    w0 = w_in[0]
    o_dk, o_dv, o_gq, o_gk, o_gv = (DIFF_WIDTH, 2 * DIFF_WIDTH, 3 * DIFF_WIDTH, 3 * DIFF_WIDTH + GQA_WIDTH,
                                    3 * DIFF_WIDTH + GQA_WIDTH + GQA_KV_WIDTH)
    group = GQA_Q_HEADS // GQA_KV_HEADS
    head_order = [j * group + t for t in range(group) for j in range(GQA_KV_HEADS)]
    gq_cols = jnp.asarray([hd * HEAD_DIM + e for hd in head_order for e in range(HEAD_DIM)], dtype=jnp.int32)
    w_gq = w0[:, o_gq:o_gk][:, gq_cols]
    w_x = jnp.concatenate([w0[:, :o_gq], w_gq, w0[:, o_gk:]], axis=1).astype(BF16)
    w_c = jnp.concatenate([w0[:, o_dk:o_gq], w0[:, o_gk:]], axis=1).astype(BF16)
    w_o = jnp.concatenate([w_out[0][:DIFF_WIDTH], w_out[0][DIFF_WIDTH:][gq_cols]], axis=0).astype(BF16)

    seg = jnp.arange(GQA_WIDTH) // HEAD_DIM
    bd = (seg[:, None] == seg[None, :]).astype(BF16)
    gqg = jnp.tile(gqa_q_norm_g[0], GQA_Q_HEADS)[None, :]
    gkg = jnp.tile(gqa_k_norm_g[0], GQA_KV_HEADS)[None, :]
    g1 = norm1_g[0][None, :]
    tables = _rope_tables(s)

    dq, gq, dkx, gkx, dvx, gvx = _inproj_call(x, mod3, lambda bi, i: (bi, 0, 0), g1, w_x, tables, bd, gqg, gkg,
                                              with_q=True, tile=IN_TILE)
    ctx_tile = ctx.shape[1]
    ctx_tables = tuple(a[:ctx_tile] for a in tables)
    dkc, gkc, dvc, gvc = _inproj_call(ctx, mod3, lambda bi, i: (b, 0, 0), g1, w_c, ctx_tables, bd, gqg, gkg,
                                      with_q=False, tile=ctx_tile)

    lamv = jnp.stack([diff_lq1[0], diff_lk1[0], diff_lq2[0], diff_lk2[0]], axis=0).astype(F32)
    subg = diff_subln_g[0][:, None]
    o = _attn_call(dq, gq, dkx, gkx, dvx, gvx, dkc, gkc, dvc, gvc, lamv, subg)

    x1, h2 = _out_call(o, w_o, x, mod3, norm2_g[0][None, :])

    t = b * s
    h2f = h2.reshape(t, d)
    wqt = peer_wq[0].T.astype(BF16)
    sk = peer_subkeys[0].reshape(PEER_HEADS * 2, PEER_N_KEYS, PEER_HALF).astype(BF16)
    half_bits = lambda a: lax.bitcast_convert_type(a.astype(BF16), jnp.uint16).astype(jnp.uint32)
    uv = (half_bits(peer_u[0]) | (half_bits(peer_v[0]) << 16))[:, None, :]
    out = _peer_call(h2f, x1.reshape(t, d), mod3, final_norm_g[None, :], wqt, sk, uv, s)
    return out.reshape(b, s, d)
```

```python
import functools
import math

import jax
import jax.numpy as jnp
from jax import lax
from jax.experimental import pallas as pl
from jax.experimental.pallas import tpu as pltpu

F32 = jnp.float32
BF16 = jnp.bfloat16
I32 = jnp.int32

HEAD_DIM = 64
GRID_W = 64
ROPE_THETA = 10000.0
NORM_EPS = 1e-6
DIFF_HEADS = 4
DIFF_V_DIM = 2 * HEAD_DIM
DIFF_WIDTH = DIFF_HEADS * DIFF_V_DIM
GQA_Q_HEADS = 8
GQA_KV_HEADS = 2
GQA_WIDTH = GQA_Q_HEADS * HEAD_DIM
GQA_KV_WIDTH = GQA_KV_HEADS * HEAD_DIM
PEER_HEADS = 8
PEER_N_KEYS = 128
PEER_HALF = 128
PEER_TOPK = 16
LAMBDA_INIT = 0.8 - 0.6 * math.exp(-0.3 * 0)
LANES = 128
VMEM_LIMIT = 56 * 1024 * 1024

IN_TILE = 1024
Q_TILE = 512
OUT_TILE = 256
SEL_TILE = 128
PEER_TILE = 16
PEER_GROUP = SEL_TILE // PEER_TILE
ATTN_GROUP = 8


def _cparams(sem):
    return pltpu.CompilerParams(dimension_semantics=sem, vmem_limit_bytes=VMEM_LIMIT)


def _split_bf16(a):
    hi = a.astype(BF16)
    lo = (a - hi.astype(F32)).astype(BF16)
    return hi, lo


def _mod_kernel(c_ref, w_ref, b_ref, o_ref):
    s = jax.nn.silu(c_ref[...])
    s_hi, s_lo = _split_bf16(s)
    w_hi, w_lo = _split_bf16(w_ref[...])
    acc = jnp.dot(s_hi, w_hi, preferred_element_type=F32)
    acc += jnp.dot(s_hi, w_lo, preferred_element_type=F32)
    acc += jnp.dot(s_lo, w_hi, preferred_element_type=F32)
    o_ref[...] = acc + b_ref[...]


def _mod_call(cc, w_mod, b_mod):
    rows, d = cc.shape
    n = w_mod.shape[1]
    tn = n // 4
    return pl.pallas_call(
        _mod_kernel,
        grid=(n // tn,),
        in_specs=[pl.BlockSpec((rows, d), lambda j: (0, 0)),
                  pl.BlockSpec((d, tn), lambda j: (0, j)),
                  pl.BlockSpec((1, tn), lambda j: (0, j))],
        out_specs=pl.BlockSpec((rows, tn), lambda j: (0, j)),
        out_shape=jax.ShapeDtypeStruct((rows, n), F32),
        compiler_params=_cparams(("arbitrary",)),
        name="mod",
    )(cc, w_mod, b_mod)


def _rope_tile(x, cos, sa, sb):
    return x * cos + pltpu.roll(x, LANES - 16, 1) * sa + pltpu.roll(x, 16, 1) * sb


def _head_rms(x, bd, g):
    sq = x * x
    hi, lo = _split_bf16(sq)
    ssum = jnp.dot(hi, bd, preferred_element_type=F32) + jnp.dot(lo, bd, preferred_element_type=F32)
    return x * lax.rsqrt(ssum * (1.0 / HEAD_DIM) + NORM_EPS) * g


def _inproj_kernel(x_ref, mod_ref, g1_ref, w_ref, cos_ref, sa_ref, sb_ref, bd_ref, gqg_ref, gkg_ref,
                   *out_refs, with_q, with_rope):
    x = x_ref[0]
    ms = jnp.mean(x * x, axis=-1, keepdims=True)
    y = x * lax.rsqrt(ms + NORM_EPS) * g1_ref[...]
    h = (y * (1.0 + mod_ref[0, 1:2, :]) + mod_ref[0, 0:1, :]).astype(BF16)
    p = jnp.dot(h, w_ref[...], preferred_element_type=F32)

    if with_rope:
        cos, sa, sb = cos_ref[...], sa_ref[...], sb_ref[...]
        rope = lambda t: _rope_tile(t, cos, sa, sb)
    else:
        rope = lambda t: t
    bd = bd_ref[...]
    scale = HEAD_DIM ** -0.5

    if with_q:
        dq_ref, gq_ref, dk_ref, gk_ref, dvt_ref, gvt_ref = out_refs
        off = 0
        for t in range(DIFF_WIDTH // LANES):
            sl = slice(t * LANES, (t + 1) * LANES)
            dq_ref[0, :, sl] = (rope(p[:, off + t * LANES: off + (t + 1) * LANES]) * scale).astype(BF16)
        off += DIFF_WIDTH
    else:
        dk_ref, gk_ref, dvt_ref, gvt_ref = out_refs
        off = 0
    for t in range(DIFF_WIDTH // LANES):
        sl = slice(t * LANES, (t + 1) * LANES)
        dk_ref[0, :, sl] = rope(p[:, off + t * LANES: off + (t + 1) * LANES]).astype(BF16)
    off += DIFF_WIDTH
    dvt_ref[0, 0] = p[:, off: off + DIFF_WIDTH].T.astype(BF16)
    off += DIFF_WIDTH
    if with_q:
        gqn = _head_rms(p[:, off: off + GQA_WIDTH], bd, gqg_ref[...])
        for t in range(GQA_WIDTH // LANES):
            sl = slice(t * LANES, (t + 1) * LANES)
            gq_ref[0, :, sl] = (rope(gqn[:, sl]) * scale).astype(BF16)
        off += GQA_WIDTH
    gkn = _head_rms(p[:, off: off + GQA_KV_WIDTH], bd[:GQA_KV_WIDTH, :GQA_KV_WIDTH], gkg_ref[...])
    gk_ref[0] = rope(gkn).astype(BF16)
    off += GQA_KV_WIDTH
    gvt_ref[0, 0] = p[:, off: off + GQA_KV_WIDTH].T.astype(BF16)


def _inproj_call(x, mod3, mod_row, g1, w, tables, bd, gqg, gkg, *, with_q, tile):
    b, s, d = x.shape
    n = w.shape[1]
    nt = s // tile
    cos, sa, sb = tables
    kv_shapes = [jax.ShapeDtypeStruct((b, s, DIFF_WIDTH), BF16),
                 jax.ShapeDtypeStruct((b, s, GQA_KV_WIDTH), BF16),
                 jax.ShapeDtypeStruct((b, nt, DIFF_WIDTH, tile), BF16),
                 jax.ShapeDtypeStruct((b, nt, GQA_KV_WIDTH, tile), BF16)]
    kv_specs = [pl.BlockSpec((1, tile, DIFF_WIDTH), lambda bi, i: (bi, i, 0)),
                pl.BlockSpec((1, tile, GQA_KV_WIDTH), lambda bi, i: (bi, i, 0)),
                pl.BlockSpec((1, 1, DIFF_WIDTH, tile), lambda bi, i: (bi, i, 0, 0)),
                pl.BlockSpec((1, 1, GQA_KV_WIDTH, tile), lambda bi, i: (bi, i, 0, 0))]
    if with_q:
        out_shapes = [jax.ShapeDtypeStruct((b, s, DIFF_WIDTH), BF16),
                      jax.ShapeDtypeStruct((b, s, GQA_WIDTH), BF16)] + kv_shapes
        out_specs = [pl.BlockSpec((1, tile, DIFF_WIDTH), lambda bi, i: (bi, i, 0)),
                     pl.BlockSpec((1, tile, GQA_WIDTH), lambda bi, i: (bi, i, 0))] + kv_specs
    else:
        out_shapes, out_specs = kv_shapes, kv_specs
    const = lambda bi, i: (0, 0)
    return pl.pallas_call(
        functools.partial(_inproj_kernel, with_q=with_q, with_rope=with_q),
        grid=(b, nt),
        in_specs=[pl.BlockSpec((1, tile, d), lambda bi, i: (bi, i, 0)),
                  pl.BlockSpec((1, 6, d), mod_row),
                  pl.BlockSpec((1, d), const),
                  pl.BlockSpec((d, n), const),
                  pl.BlockSpec((tile, LANES), lambda bi, i: (i, 0)),
                  pl.BlockSpec((tile, LANES), lambda bi, i: (i, 0)),
                  pl.BlockSpec((tile, LANES), lambda bi, i: (i, 0)),
                  pl.BlockSpec(bd.shape, const),
                  pl.BlockSpec(gqg.shape, const),
                  pl.BlockSpec(gkg.shape, const)],
        out_specs=out_specs,
        out_shape=out_shapes,
        compiler_params=_cparams(("parallel", "parallel")),
        name="inproj_x" if with_q else "inproj_ctx",
    )(x, mod3, g1, w, cos, sa, sb, bd, gqg, gkg)


def _attn_steps(ks, qs, vts, carries):
    ss = [lax.dot_general(k, q, (((1,), (1,)), ((), ())), preferred_element_type=F32) for k, q in zip(ks, qs)]
    stats = []
    for s, (m, l, _) in zip(ss, carries):
        m_new = jnp.maximum(m, jnp.max(s, axis=0, keepdims=True))
        alpha = jnp.exp(m - m_new)
        p = jnp.exp(s - m_new)
        stats.append((m_new, alpha, alpha * l + jnp.sum(p, axis=0, keepdims=True), p.astype(BF16)))
    return tuple((m_new, l, alpha * acc + jnp.dot(vt, p, preferred_element_type=F32))
                 for (m_new, alpha, l, p), vt, (_, _, acc) in zip(stats, vts, carries))


def _attn_kernel(dq_ref, gq_ref, dkx_ref, gkx_ref, dvx_ref, gvx_ref, dkc_ref, gkc_ref, dvc_ref, gvc_ref,
                 lamv_ref, subg_ref, o_ref):
    tq = dq_ref.shape[1]
    n_chunks, chunk = dvx_ref.shape[1], dvx_ref.shape[3]
    lv = lamv_ref[...]
    lam = (jnp.exp(jnp.sum(lv[0:1] * lv[1:2], axis=-1, keepdims=True))
           - jnp.exp(jnp.sum(lv[2:3] * lv[3:4], axis=-1, keepdims=True)) + LAMBDA_INIT)
    low = lax.broadcasted_iota(I32, (tq, LANES), 1) < HEAD_DIM

    def run_units(units):
        def go(kx_ref, kc_ref, vx_ref, vc_ref, dv):
            qs = [q for q, _, _ in units]
            init = tuple((jnp.full((1, tq), -jnp.inf, F32), jnp.zeros((1, tq), F32), jnp.zeros((dv, tq), F32))
                         for _ in units)
            carries = _attn_steps([kc_ref[0, :, col:col + LANES] for _, col, _ in units], qs,
                                  [vc_ref[0, 0, r0:r0 + dv, :] for _, _, r0 in units], init)

            def body(c, carries):
                off = pl.multiple_of(c * chunk, chunk)
                return _attn_steps([kx_ref[0, pl.ds(off, chunk), col:col + LANES] for _, col, _ in units], qs,
                                   [vx_ref[0, c, r0:r0 + dv, :] for _, _, r0 in units], carries)

            return [acc / l for _, l, acc in lax.fori_loop(0, n_chunks, body, carries)]
        return go

    zero = jnp.zeros((tq, LANES), BF16)
    for h0 in range(0, DIFF_HEADS, ATTN_GROUP // 2):
        heads = range(h0, h0 + ATTN_GROUP // 2)
        units = []
        for h in heads:
            qh = dq_ref[0, :, h * LANES:(h + 1) * LANES]
            units += [(jnp.where(low, qh, zero), h * LANES, h * DIFF_V_DIM),
                      (jnp.where(low, zero, qh), h * LANES, h * DIFF_V_DIM)]
        outs = run_units(units)(dkx_ref, dkc_ref, dvx_ref, dvc_ref, DIFF_V_DIM)
        for n, h in enumerate(heads):
            od = outs[2 * n] - lam * outs[2 * n + 1]
            ms = jnp.mean(od * od, axis=0, keepdims=True)
            od = od * lax.rsqrt(ms + NORM_EPS) * subg_ref[...] * (1.0 - LAMBDA_INIT)
            o_ref[0, :, h * LANES:(h + 1) * LANES] = od.T.astype(BF16)
    for t0 in range(0, GQA_Q_HEADS // 2, ATTN_GROUP // 2):
        tiles = range(t0, t0 + ATTN_GROUP // 2)
        units = []
        for t in tiles:
            qt = gq_ref[0, :, t * LANES:(t + 1) * LANES]
            units += [(jnp.where(low, qt, zero), 0, 0), (jnp.where(low, zero, qt), 0, HEAD_DIM)]
        outs = run_units(units)(gkx_ref, gkc_ref, gvx_ref, gvc_ref, HEAD_DIM)
        for n, t in enumerate(tiles):
            og = jnp.concatenate([outs[2 * n], outs[2 * n + 1]], axis=0)
            o_ref[0, :, DIFF_WIDTH + t * LANES: DIFF_WIDTH + (t + 1) * LANES] = og.T.astype(BF16)


def _attn_call(dq, gq, dkx, gkx, dvx, gvx, dkc, gkc, dvc, gvc, lamv, subg):
    b, s, _ = dq.shape
    nq = s // Q_TILE
    full3 = lambda a: pl.BlockSpec((1,) + a.shape[1:], lambda bi, i: (bi, 0, 0))
    full4 = lambda a: pl.BlockSpec((1,) + a.shape[1:], lambda bi, i: (bi, 0, 0, 0))
    const = lambda bi, i: (0, 0)
    return pl.pallas_call(
        _attn_kernel,
        grid=(b, nq),
        in_specs=[pl.BlockSpec((1, Q_TILE, DIFF_WIDTH), lambda bi, i: (bi, i, 0)),
                  pl.BlockSpec((1, Q_TILE, GQA_WIDTH), lambda bi, i: (bi, i, 0)),
                  full3(dkx), full3(gkx), full4(dvx), full4(gvx),
                  full3(dkc), full3(gkc), full4(dvc), full4(gvc),
                  pl.BlockSpec(lamv.shape, const), pl.BlockSpec(subg.shape, const)],
        out_specs=pl.BlockSpec((1, Q_TILE, DIFF_WIDTH + GQA_WIDTH), lambda bi, i: (bi, i, 0)),
        out_shape=jax.ShapeDtypeStruct((b, s, DIFF_WIDTH + GQA_WIDTH), BF16),
        compiler_params=_cparams(("parallel", "arbitrary")),
        name="attn",
    )(dq, gq, dkx, gkx, dvx, gvx, dkc, gkc, dvc, gvc, lamv, subg)


def _out_kernel(o_ref, w_ref, x_ref, mod_ref, g2_ref, x1_ref, h2_ref):
    attn = jnp.dot(o_ref[0], w_ref[...], preferred_element_type=F32)
    x1 = x_ref[0] + mod_ref[0, 2:3, :] * attn
    x1_ref[0] = x1
    ms = jnp.mean(x1 * x1, axis=-1, keepdims=True)
    y = x1 * lax.rsqrt(ms + NORM_EPS) * g2_ref[...]
    h2_ref[0] = (y * (1.0 + mod_ref[0, 4:5, :]) + mod_ref[0, 3:4, :]).astype(BF16)


def _out_call(o, w_out, x, mod3, g2):
    b, s, d = x.shape
    const = lambda bi, i: (0, 0)
    tile = lambda w: pl.BlockSpec((1, OUT_TILE, w), lambda bi, i: (bi, i, 0))
    return pl.pallas_call(
        _out_kernel,
        grid=(b, s // OUT_TILE),
        in_specs=[tile(o.shape[2]), pl.BlockSpec(w_out.shape, const), tile(d),
                  pl.BlockSpec((1, 6, d), lambda bi, i: (bi, 0, 0)), pl.BlockSpec((1, d), const)],
        out_specs=[tile(d), tile(d)],
        out_shape=[jax.ShapeDtypeStruct((b, s, d), F32), jax.ShapeDtypeStruct((b, s, d), BF16)],
        compiler_params=_cparams(("parallel", "parallel")),
        name="outproj",
    )(o, w_out, x, mod3, g2)


def _topk_sublanes(s, flat, k):
    t = s.shape[1]
    big = jnp.iinfo(jnp.int32).max
    rowk = lax.broadcasted_iota(I32, (k, t), 0)
    vals = jnp.zeros((k, t), F32)
    labs = jnp.zeros((k, t), I32)
    for r in range(k):
        m = jnp.max(s, axis=0, keepdims=True)
        lab = jnp.min(jnp.where(s == m, flat, big), axis=0, keepdims=True)
        vals = jnp.where(rowk == r, m, vals)
        labs = jnp.where(rowk == r, lab, labs)
        s = jnp.where(flat == lab, -jnp.inf, s)
    return vals, labs


def _select_head(h, h_ref, wqt_ref, sk_ref, gates_t_ref, idst_ref):
    ts = h_ref.shape[0]
    k = PEER_TOPK
    key_iota = lax.broadcasted_iota(I32, (PEER_N_KEYS, ts), 0)
    sub8 = lax.broadcasted_iota(I32, (8, ts), 0)
    flat_rows = [lax.broadcasted_iota(I32, (k, ts), 0)] + [i * k + sub8 for i in range(1, k)]
    cand_flat = jnp.concatenate(flat_rows, axis=0)
    r0 = pl.multiple_of(h * 2 * PEER_HALF, 2 * PEER_HALF)
    qt = lax.dot_general(wqt_ref[pl.ds(r0, 2 * PEER_HALF), :], h_ref[...], (((1,), (1,)), ((), ())),
                         preferred_element_type=F32).astype(BF16)
    tops = []
    for p in range(2):
        q = qt[p * PEER_HALF:(p + 1) * PEER_HALF, :]
        s = jnp.dot(sk_ref[2 * h + p], q, preferred_element_type=F32)
        tops.append(_topk_sublanes(s, key_iota, k))
    (v1, i1), (v2, i2) = tops
    vals = [v1[0:1] + v2] + [v1[i:i + 1] + v2[:8] for i in range(1, k)]
    eids = [i1[0:1] * PEER_N_KEYS + i2] + [i1[i:i + 1] * PEER_N_KEYS + i2[:8] for i in range(1, k)]
    cand = jnp.concatenate(vals, axis=0)
    cid = jnp.concatenate(eids, axis=0)
    best, pos = _topk_sublanes(cand, cand_flat, k)
    rowk = lax.broadcasted_iota(I32, (k, ts), 0)
    ids = jnp.zeros((k, ts), I32)
    for r in range(k):
        pick = jnp.sum(jnp.where(cand_flat == pos[r:r + 1], cid, 0), axis=0, keepdims=True)
        ids = jnp.where(rowk == r, pick, ids)
    e = jnp.exp(best - best[0:1])
    o0 = pl.multiple_of(h * k, k)
    gates_t_ref[pl.ds(o0, k), :] = e / jnp.sum(e, axis=0, keepdims=True)
    idst_ref[pl.ds(o0, k), :] = ids


def _peer_kernel(h_cur_ref, h_sel_ref, h_s0_ref, h_s1_ref, x1_ref, mod_ref, fg_ref, wqt_ref, sk_ref, uv_hbm, o_ref,
                 buf0, buf1, sem, ids_s, gates_s, idst_ref, idtok_ref, gates_t_ref, mix_ref, sem_s,
                 *, nsteps):
    tt = PEER_TILE
    nsel = PEER_HEADS * PEER_TOPK
    d = h_cur_ref.shape[1]
    npair = PEER_GROUP // 2
    i = pl.program_id(0)
    bufs = (buf0, buf1)
    r_cur, r_next, r_sel = lax.rem(i, 3), lax.rem(i + 1, 3), lax.rem(i + 2, 3)

    def row_copy(ring, tok, t, e, sl):
        return pltpu.make_async_copy(uv_hbm.at[ids_s[ring, tok, e]], bufs[sl].at[pl.ds(t * nsel + e, 1)],
                                     sem.at[sl])

    def issue(ring, tok0, sl):
        for t in range(tt):
            for e in range(nsel):
                row_copy(ring, tok0 + t, t, e, sl).start(priority=e % 2)

    def wait_slot(sl):
        pltpu.make_async_copy(bufs[1 - sl], bufs[sl], sem.at[sl]).wait()

    def mix(sl, k):
        buf = bufs[sl]
        t0 = pl.multiple_of(k * tt, tt)
        hx = h_cur_ref[pl.ds(t0, tt), :].astype(F32)
        lane_t = lax.broadcasted_iota(I32, (nsel, tt), 1)
        a = jnp.zeros((nsel, tt), F32)
        for t in range(tt):
            u = pltpu.bitcast(buf[t * nsel:(t + 1) * nsel, :] << 16, F32)
            prod = u * hx[t:t + 1, :]
            part = prod[:, 0:LANES]
            for c in range(1, d // LANES):
                part = part + prod[:, c * LANES:(c + 1) * LANES]
            a = jnp.where(lane_t == t, jnp.sum(part, axis=1, keepdims=True), a)
        gelu = 0.5 * a * (1.0 + lax.erf(a * (2.0 ** -0.5)))
        w = gates_s[r_cur, k] * gelu
        hi_mask = jnp.uint32(0xFFFF0000)
        for t in range(tt):
            v = pltpu.bitcast(buf[t * nsel:(t + 1) * nsel, :] & hi_mask, F32)
            mix_ref[pl.ds(t0 + t, 1), :] = jnp.sum(v * w[:, t:t + 1], axis=0, keepdims=True)

    def publish(ring):
        idtok_ref[...] = idst_ref[...].T
        cp = pltpu.make_async_copy(idtok_ref, ids_s.at[ring], sem_s.at[0])
        cp.start()
        cp.wait()
        for k in range(PEER_GROUP):
            gates_s[ring, k] = gates_t_ref[:, k * tt:(k + 1) * tt]

    def select_all(h_ref, ring):
        def head(h, c):
            _select_head(h, h_ref, wqt_ref, sk_ref, gates_t_ref, idst_ref)
            return c
        lax.fori_loop(0, PEER_HEADS, head, 0)
        publish(ring)

    @pl.when(i == 0)
    def _():
        select_all(h_s0_ref, 0)
        select_all(h_s1_ref, 1)

        def body(t, c):
            for e in range(nsel):
                row_copy(0, t, t, e, 0).start(priority=e % 2)
            return c
        lax.fori_loop(0, tt, body, 0)

    assert PEER_HEADS == 2 * npair

    def pair(p, c):
        k0 = 2 * p
        last = p == npair - 1
        wait_slot(0)
        issue(r_cur, (k0 + 1) * tt, 1)
        mix(0, k0)
        _select_head(2 * p, h_sel_ref, wqt_ref, sk_ref, gates_t_ref, idst_ref)
        wait_slot(1)
        issue(jnp.where(last, r_next, r_cur), jnp.where(last, 0, (k0 + 2) * tt), 0)
        mix(1, k0 + 1)
        _select_head(2 * p + 1, h_sel_ref, wqt_ref, sk_ref, gates_t_ref, idst_ref)

        @pl.when(last)
        def _():
            publish(r_sel)
        return c

    lax.fori_loop(0, npair, pair, 0)

    y = x1_ref[...] + mod_ref[0, 5:6, :] * mix_ref[...]
    ms = jnp.mean(y * y, axis=-1, keepdims=True)
    o_ref[...] = y * lax.rsqrt(ms + NORM_EPS) * fg_ref[...]

    @pl.when(i == nsteps - 1)
    def _():
        wait_slot(0)


def _peer_call(h2, x1, mod3, fg, wqt, sk, uv, seq):
    t, d = h2.shape
    nsel = PEER_HEADS * PEER_TOPK
    tt = PEER_TILE
    rows_per_step = PEER_GROUP * tt
    assert rows_per_step == SEL_TILE and seq % rows_per_step == 0 and PEER_HEADS % (PEER_GROUP // 2) == 0
    nsteps = t // rows_per_step
    assert nsteps >= 2
    per_batch = seq // rows_per_step
    rows = lambda f: pl.BlockSpec((rows_per_step, d), f)
    return pl.pallas_call(
        functools.partial(_peer_kernel, nsteps=nsteps),
        grid=(nsteps,),
        in_specs=[rows(lambda i: (i, 0)),
                  rows(lambda i: (jnp.minimum(i + 2, nsteps - 1), 0)),
                  rows(lambda i: (0, 0)),
                  rows(lambda i: (1, 0)),
                  rows(lambda i: (i, 0)),
                  pl.BlockSpec((1, 6, d), lambda i: (i // per_batch, 0, 0)),
                  pl.BlockSpec((1, d), lambda i: (0, 0)),
                  pl.BlockSpec(wqt.shape, lambda i: (0, 0)),
                  pl.BlockSpec(sk.shape, lambda i: (0, 0, 0)),
                  pl.BlockSpec(memory_space=pl.ANY)],
        out_specs=rows(lambda i: (i, 0)),
        out_shape=jax.ShapeDtypeStruct((t, d), F32),
        scratch_shapes=[pltpu.VMEM((tt * nsel, d), jnp.uint32),
                        pltpu.VMEM((tt * nsel, d), jnp.uint32),
                        pltpu.SemaphoreType.DMA((2,)),
                        pltpu.SMEM((3, rows_per_step, nsel), I32),
                        pltpu.VMEM((3, PEER_GROUP, nsel, tt), F32),
                        pltpu.VMEM((nsel, rows_per_step), I32),
                        pltpu.VMEM((rows_per_step, nsel), I32),
                        pltpu.VMEM((nsel, rows_per_step), F32),
                        pltpu.VMEM((rows_per_step, d), F32),
                        pltpu.SemaphoreType.DMA((1,))],
        compiler_params=_cparams(("arbitrary",)),
        name="peer",
    )(h2, h2, h2, h2, x1, mod3, fg, wqt, sk, uv)


def _rope_tables(seq):
    half = HEAD_DIM // 4
    freqs = ROPE_THETA ** (-jnp.arange(half, dtype=F32) / half)
    pos = jnp.arange(seq, dtype=jnp.int32)
    ang_r = (pos // GRID_W).astype(F32)[:, None] * freqs[None, :]
    ang_c = (pos % GRID_W).astype(F32)[:, None] * freqs[None, :]
    z = jnp.zeros_like(ang_r)
    cos = jnp.concatenate([jnp.cos(ang_r)] * 2 + [jnp.cos(ang_c)] * 2, axis=1)
    sa = jnp.concatenate([-jnp.sin(ang_r), z, -jnp.sin(ang_c), z], axis=1)
    sb = jnp.concatenate([z, jnp.sin(ang_r), z, jnp.sin(ang_c)], axis=1)
    rep = LANES // HEAD_DIM
    return tuple(jnp.tile(a, (1, rep)) for a in (cos, sa, sb))


def kernel(x, c, ctx, c_ctx, w_mod, b_mod, norm1_g, norm2_g, w_in, w_out, diff_lq1, diff_lk1, diff_lq2,
           diff_lk2, diff_subln_g, gqa_q_norm_g, gqa_k_norm_g, peer_wq, peer_subkeys, peer_u, peer_v,
           final_norm_g):
    b, s, d = x.shape
    assert w_mod.shape[0] == 1, "depth-1 block"
    assert s % IN_TILE == 0 and s % Q_TILE == 0 and ctx.shape[1] % LANES == 0

    pad = (-(b + 1)) % 8
    cc = jnp.concatenate([c, c_ctx[None, :], jnp.zeros((pad, d), F32)], axis=0)
    mod3 = _mod_call(cc, w_mod[0], b_mod[0][None, :]).reshape(cc.shape[0], 6, d)

    w0 = w_in[0]
    o_dk, o_dv, o_gq, o_gk, o_gv = (DIFF_WIDTH, 2 * DIFF_WIDTH, 3 * DIFF_WIDTH, 3 * DIFF_WIDTH + GQA_WIDTH,
                                    3 * DIFF_WIDTH + GQA_WIDTH + GQA_KV_WIDTH)
    group = GQA_Q_HEADS // GQA_KV_HEADS
    head_order = [j * group + t for t in range(group) for j in range(GQA_KV_HEADS)]
    gq_cols = jnp.asarray([hd * HEAD_DIM + e for hd in head_order for e in range(HEAD_DIM)], dtype=jnp.int32)
    w_gq = w0[:, o_gq:o_gk][:, gq_cols]
    w_x = jnp.concatenate([w0[:, :o_gq], w_gq, w0[:, o_gk:]], axis=1).astype(BF16)
    w_c = jnp.concatenate([w0[:, o_dk:o_gq], w0[:, o_gk:]], axis=1).astype(BF16)
    w_o = jnp.concatenate([w_out[0][:DIFF_WIDTH], w_out[0][DIFF_WIDTH:][gq_cols]], axis=0).astype(BF16)

    seg = jnp.arange(GQA_WIDTH) // HEAD_DIM
    bd = (seg[:, None] == seg[None, :]).astype(BF16)
    gqg = jnp.tile(gqa_q_norm_g[0], GQA_Q_HEADS)[None, :]
    gkg = jnp.tile(gqa_k_norm_g[0], GQA_KV_HEADS)[None, :]
    g1 = norm1_g[0][None, :]
    tables = _rope_tables(s)

    dq, gq, dkx, gkx, dvx, gvx = _inproj_call(x, mod3, lambda bi, i: (bi, 0, 0), g1, w_x, tables, bd, gqg, gkg,
                                              with_q=True, tile=IN_TILE)
    ctx_tile = ctx.shape[1]
    ctx_tables = tuple(a[:ctx_tile] for a in tables)
    dkc, gkc, dvc, gvc = _inproj_call(ctx, mod3, lambda bi, i: (b, 0, 0), g1, w_c, ctx_tables, bd, gqg, gkg,
                                      with_q=False, tile=ctx_tile)

    lamv = jnp.stack([diff_lq1[0], diff_lk1[0], diff_lq2[0], diff_lk2[0]], axis=0).astype(F32)
    subg = diff_subln_g[0][:, None]
    o = _attn_call(dq, gq, dkx, gkx, dvx, gvx, dkc, gkc, dvc, gvc, lamv, subg)

    x1, h2 = _out_call(o, w_o, x, mod3, norm2_g[0][None, :])

    t = b * s
    h2f = h2.reshape(t, d)
    wqt = peer_wq[0].T.astype(BF16)
    sk = peer_subkeys[0].reshape(PEER_HEADS * 2, PEER_N_KEYS, PEER_HALF).astype(BF16)
    half_bits = lambda a: lax.bitcast_convert_type(a.astype(BF16), jnp.uint16).astype(jnp.uint32)
    uv = (half_bits(peer_u[0]) | (half_bits(peer_v[0]) << 16))[:, None, :]
    out = _peer_call(h2f, x1.reshape(t, d), mod3, final_norm_g[None, :], wqt, sk, uv, s)
    return out.reshape(b, s, d)
```

```python
import functools
import math

import jax
import jax.numpy as jnp
from jax import lax
from jax.experimental import pallas as pl
from jax.experimental.pallas import tpu as pltpu

F32 = jnp.float32
BF16 = jnp.bfloat16
I32 = jnp.int32

HEAD_DIM = 64
GRID_W = 64
ROPE_THETA = 10000.0
NORM_EPS = 1e-6
DIFF_HEADS = 4
DIFF_V_DIM = 2 * HEAD_DIM
DIFF_WIDTH = DIFF_HEADS * DIFF_V_DIM
GQA_Q_HEADS = 8
GQA_KV_HEADS = 2
GQA_WIDTH = GQA_Q_HEADS * HEAD_DIM
GQA_KV_WIDTH = GQA_KV_HEADS * HEAD_DIM
PEER_HEADS = 8
PEER_N_KEYS = 128
PEER_HALF = 128
PEER_TOPK = 16
LAMBDA_INIT = 0.8 - 0.6 * math.exp(-0.3 * 0)
LANES = 128
VMEM_LIMIT = 56 * 1024 * 1024

IN_TILE = 1024
Q_TILE = 512
OUT_TILE = 256
SEL_TILE = 128
PEER_TILE = 16
PEER_GROUP = SEL_TILE // PEER_TILE
ATTN_GROUP = 8


def _cparams(sem):
    return pltpu.CompilerParams(dimension_semantics=sem, vmem_limit_bytes=VMEM_LIMIT)


def _split_bf16(a):
    hi = a.astype(BF16)
    lo = (a - hi.astype(F32)).astype(BF16)
    return hi, lo


def _mod_kernel(c_ref, w_ref, b_ref, o_ref):
    s = jax.nn.silu(c_ref[...])
    s_hi, s_lo = _split_bf16(s)
    w_hi, w_lo = _split_bf16(w_ref[...])
    acc = jnp.dot(s_hi, w_hi, preferred_element_type=F32)
    acc += jnp.dot(s_hi, w_lo, preferred_element_type=F32)
    acc += jnp.dot(s_lo, w_hi, preferred_element_type=F32)
    o_ref[...] = acc + b_ref[...]


def _mod_call(cc, w_mod, b_mod):
    rows, d = cc.shape
    n = w_mod.shape[1]
    tn = n // 4
    return pl.pallas_call(
        _mod_kernel,
        grid=(n // tn,),
        in_specs=[pl.BlockSpec((rows, d), lambda j: (0, 0)),
                  pl.BlockSpec((d, tn), lambda j: (0, j)),
                  pl.BlockSpec((1, tn), lambda j: (0, j))],
        out_specs=pl.BlockSpec((rows, tn), lambda j: (0, j)),
        out_shape=jax.ShapeDtypeStruct((rows, n), F32),
        compiler_params=_cparams(("arbitrary",)),
        name="mod",
    )(cc, w_mod, b_mod)


def _rope_tile(x, cos, sa, sb):
    return x * cos + pltpu.roll(x, LANES - 16, 1) * sa + pltpu.roll(x, 16, 1) * sb


def _head_rms(x, bd, g):
    sq = x * x
    hi, lo = _split_bf16(sq)
    ssum = jnp.dot(hi, bd, preferred_element_type=F32) + jnp.dot(lo, bd, preferred_element_type=F32)
    return x * lax.rsqrt(ssum * (1.0 / HEAD_DIM) + NORM_EPS) * g


def _inproj_kernel(x_ref, mod_ref, g1_ref, w_ref, cos_ref, sa_ref, sb_ref, bd_ref, gqg_ref, gkg_ref,
                   *out_refs, with_q, with_rope):
    x = x_ref[0]
    ms = jnp.mean(x * x, axis=-1, keepdims=True)
    y = x * lax.rsqrt(ms + NORM_EPS) * g1_ref[...]
    h = (y * (1.0 + mod_ref[0, 1:2, :]) + mod_ref[0, 0:1, :]).astype(BF16)
    p = jnp.dot(h, w_ref[...], preferred_element_type=F32)

    if with_rope:
        cos, sa, sb = cos_ref[...], sa_ref[...], sb_ref[...]
        rope = lambda t: _rope_tile(t, cos, sa, sb)
    else:
        rope = lambda t: t
    bd = bd_ref[...]
    scale = HEAD_DIM ** -0.5

    if with_q:
        dq_ref, gq_ref, dk_ref, gk_ref, dvt_ref, gvt_ref = out_refs
        off = 0
        for t in range(DIFF_WIDTH // LANES):
            sl = slice(t * LANES, (t + 1) * LANES)
            dq_ref[0, :, sl] = (rope(p[:, off + t * LANES: off + (t + 1) * LANES]) * scale).astype(BF16)
        off += DIFF_WIDTH
    else:
        dk_ref, gk_ref, dvt_ref, gvt_ref = out_refs
        off = 0
    for t in range(DIFF_WIDTH // LANES):
        sl = slice(t * LANES, (t + 1) * LANES)
        dk_ref[0, :, sl] = rope(p[:, off + t * LANES: off + (t + 1) * LANES]).astype(BF16)
    off += DIFF_WIDTH
    dvt_ref[0, 0] = p[:, off: off + DIFF_WIDTH].T.astype(BF16)
    off += DIFF_WIDTH
    if with_q:
        gqn = _head_rms(p[:, off: off + GQA_WIDTH], bd, gqg_ref[...])
        for t in range(GQA_WIDTH // LANES):
            sl = slice(t * LANES, (t + 1) * LANES)
            gq_ref[0, :, sl] = (rope(gqn[:, sl]) * scale).astype(BF16)
        off += GQA_WIDTH
    gkn = _head_rms(p[:, off: off + GQA_KV_WIDTH], bd[:GQA_KV_WIDTH, :GQA_KV_WIDTH], gkg_ref[...])
    gk_ref[0] = rope(gkn).astype(BF16)
    off += GQA_KV_WIDTH
    gvt_ref[0, 0] = p[:, off: off + GQA_KV_WIDTH].T.astype(BF16)


def _inproj_call(x, mod3, mod_row, g1, w, tables, bd, gqg, gkg, *, with_q, tile):
    b, s, d = x.shape
    n = w.shape[1]
    nt = s // tile
    cos, sa, sb = tables
    kv_shapes = [jax.ShapeDtypeStruct((b, s, DIFF_WIDTH), BF16),
                 jax.ShapeDtypeStruct((b, s, GQA_KV_WIDTH), BF16),
                 jax.ShapeDtypeStruct((b, nt, DIFF_WIDTH, tile), BF16),
                 jax.ShapeDtypeStruct((b, nt, GQA_KV_WIDTH, tile), BF16)]
    kv_specs = [pl.BlockSpec((1, tile, DIFF_WIDTH), lambda bi, i: (bi, i, 0)),
                pl.BlockSpec((1, tile, GQA_KV_WIDTH), lambda bi, i: (bi, i, 0)),
                pl.BlockSpec((1, 1, DIFF_WIDTH, tile), lambda bi, i: (bi, i, 0, 0)),
                pl.BlockSpec((1, 1, GQA_KV_WIDTH, tile), lambda bi, i: (bi, i, 0, 0))]
    if with_q:
        out_shapes = [jax.ShapeDtypeStruct((b, s, DIFF_WIDTH), BF16),
                      jax.ShapeDtypeStruct((b, s, GQA_WIDTH), BF16)] + kv_shapes
        out_specs = [pl.BlockSpec((1, tile, DIFF_WIDTH), lambda bi, i: (bi, i, 0)),
                     pl.BlockSpec((1, tile, GQA_WIDTH), lambda bi, i: (bi, i, 0))] + kv_specs
    else:
        out_shapes, out_specs = kv_shapes, kv_specs
    const = lambda bi, i: (0, 0)
    return pl.pallas_call(
        functools.partial(_inproj_kernel, with_q=with_q, with_rope=with_q),
        grid=(b, nt),
        in_specs=[pl.BlockSpec((1, tile, d), lambda bi, i: (bi, i, 0)),
                  pl.BlockSpec((1, 6, d), mod_row),
                  pl.BlockSpec((1, d), const),
                  pl.BlockSpec((d, n), const),
                  pl.BlockSpec((tile, LANES), lambda bi, i: (i, 0)),
                  pl.BlockSpec((tile, LANES), lambda bi, i: (i, 0)),
                  pl.BlockSpec((tile, LANES), lambda bi, i: (i, 0)),
                  pl.BlockSpec(bd.shape, const),
                  pl.BlockSpec(gqg.shape, const),
                  pl.BlockSpec(gkg.shape, const)],
        out_specs=out_specs,
        out_shape=out_shapes,
        compiler_params=_cparams(("parallel", "parallel")),
        name="inproj_x" if with_q else "inproj_ctx",
    )(x, mod3, g1, w, cos, sa, sb, bd, gqg, gkg)


def _attn_steps(ks, qs, vts, carries):
    ss = [lax.dot_general(k, q, (((1,), (1,)), ((), ())), preferred_element_type=F32) for k, q in zip(ks, qs)]
    stats = []
    for s, (m, l, _) in zip(ss, carries):
        m_new = jnp.maximum(m, jnp.max(s, axis=0, keepdims=True))
        alpha = jnp.exp(m - m_new)
        p = jnp.exp(s - m_new)
        stats.append((m_new, alpha, alpha * l + jnp.sum(p, axis=0, keepdims=True), p.astype(BF16)))
    return tuple((m_new, l, alpha * acc + jnp.dot(vt, p, preferred_element_type=F32))
                 for (m_new, alpha, l, p), vt, (_, _, acc) in zip(stats, vts, carries))


def _attn_kernel(dq_ref, gq_ref, dkx_ref, gkx_ref, dvx_ref, gvx_ref, dkc_ref, gkc_ref, dvc_ref, gvc_ref,
                 lamv_ref, subg_ref, o_ref):
    tq = dq_ref.shape[1]
    n_chunks, chunk = dvx_ref.shape[1], dvx_ref.shape[3]
    lv = lamv_ref[...]
    lam = (jnp.exp(jnp.sum(lv[0:1] * lv[1:2], axis=-1, keepdims=True))
           - jnp.exp(jnp.sum(lv[2:3] * lv[3:4], axis=-1, keepdims=True)) + LAMBDA_INIT)
    low = lax.broadcasted_iota(I32, (tq, LANES), 1) < HEAD_DIM

    def run_units(units):
        def go(kx_ref, kc_ref, vx_ref, vc_ref, dv):
            qs = [q for q, _, _ in units]
            init = tuple((jnp.full((1, tq), -jnp.inf, F32), jnp.zeros((1, tq), F32), jnp.zeros((dv, tq), F32))
                         for _ in units)
            carries = _attn_steps([kc_ref[0, :, col:col + LANES] for _, col, _ in units], qs,
                                  [vc_ref[0, 0, r0:r0 + dv, :] for _, _, r0 in units], init)

            def body(c, carries):
                off = pl.multiple_of(c * chunk, chunk)
                return _attn_steps([kx_ref[0, pl.ds(off, chunk), col:col + LANES] for _, col, _ in units], qs,
                                   [vx_ref[0, c, r0:r0 + dv, :] for _, _, r0 in units], carries)

            return [acc / l for _, l, acc in lax.fori_loop(0, n_chunks, body, carries)]
        return go

    zero = jnp.zeros((tq, LANES), BF16)
    for h0 in range(0, DIFF_HEADS, ATTN_GROUP // 2):
        heads = range(h0, h0 + ATTN_GROUP // 2)
        units = []
        for h in heads:
            qh = dq_ref[0, :, h * LANES:(h + 1) * LANES]
            units += [(jnp.where(low, qh, zero), h * LANES, h * DIFF_V_DIM),
                      (jnp.where(low, zero, qh), h * LANES, h * DIFF_V_DIM)]
        outs = run_units(units)(dkx_ref, dkc_ref, dvx_ref, dvc_ref, DIFF_V_DIM)
        for n, h in enumerate(heads):
            od = outs[2 * n] - lam * outs[2 * n + 1]
            ms = jnp.mean(od * od, axis=0, keepdims=True)
            od = od * lax.rsqrt(ms + NORM_EPS) * subg_ref[...] * (1.0 - LAMBDA_INIT)
            o_ref[0, :, h * LANES:(h + 1) * LANES] = od.T.astype(BF16)
    for t0 in range(0, GQA_Q_HEADS // 2, ATTN_GROUP // 2):
        tiles = range(t0, t0 + ATTN_GROUP // 2)
        units = []
        for t in tiles:
            qt = gq_ref[0, :, t * LANES:(t + 1) * LANES]
            units += [(jnp.where(low, qt, zero), 0, 0), (jnp.where(low, zero, qt), 0, HEAD_DIM)]
        outs = run_units(units)(gkx_ref, gkc_ref, gvx_ref, gvc_ref, HEAD_DIM)
        for n, t in enumerate(tiles):
            og = jnp.concatenate([outs[2 * n], outs[2 * n + 1]], axis=0)
            o_ref[0, :, DIFF_WIDTH + t * LANES: DIFF_WIDTH + (t + 1) * LANES] = og.T.astype(BF16)


def _attn_call(dq, gq, dkx, gkx, dvx, gvx, dkc, gkc, dvc, gvc, lamv, subg):
    b, s, _ = dq.shape
    nq = s // Q_TILE
    full3 = lambda a: pl.BlockSpec((1,) + a.shape[1:], lambda bi, i: (bi, 0, 0))
    full4 = lambda a: pl.BlockSpec((1,) + a.shape[1:], lambda bi, i: (bi, 0, 0, 0))
    const = lambda bi, i: (0, 0)
    return pl.pallas_call(
        _attn_kernel,
        grid=(b, nq),
        in_specs=[pl.BlockSpec((1, Q_TILE, DIFF_WIDTH), lambda bi, i: (bi, i, 0)),
                  pl.BlockSpec((1, Q_TILE, GQA_WIDTH), lambda bi, i: (bi, i, 0)),
                  full3(dkx), full3(gkx), full4(dvx), full4(gvx),
                  full3(dkc), full3(gkc), full4(dvc), full4(gvc),
                  pl.BlockSpec(lamv.shape, const), pl.BlockSpec(subg.shape, const)],
        out_specs=pl.BlockSpec((1, Q_TILE, DIFF_WIDTH + GQA_WIDTH), lambda bi, i: (bi, i, 0)),
        out_shape=jax.ShapeDtypeStruct((b, s, DIFF_WIDTH + GQA_WIDTH), BF16),
        compiler_params=_cparams(("parallel", "arbitrary")),
        name="attn",
    )(dq, gq, dkx, gkx, dvx, gvx, dkc, gkc, dvc, gvc, lamv, subg)


def _out_kernel(o_ref, w_ref, x_ref, mod_ref, g2_ref, x1_ref, h2_ref):
    attn = jnp.dot(o_ref[0], w_ref[...], preferred_element_type=F32)
    x1 = x_ref[0] + mod_ref[0, 2:3, :] * attn
    x1_ref[0] = x1
    ms = jnp.mean(x1 * x1, axis=-1, keepdims=True)
    y = x1 * lax.rsqrt(ms + NORM_EPS) * g2_ref[...]
    h2_ref[0] = (y * (1.0 + mod_ref[0, 4:5, :]) + mod_ref[0, 3:4, :]).astype(BF16)


def _out_call(o, w_out, x, mod3, g2):
    b, s, d = x.shape
    const = lambda bi, i: (0, 0)
    tile = lambda w: pl.BlockSpec((1, OUT_TILE, w), lambda bi, i: (bi, i, 0))
    return pl.pallas_call(
        _out_kernel,
        grid=(b, s // OUT_TILE),
        in_specs=[tile(o.shape[2]), pl.BlockSpec(w_out.shape, const), tile(d),
                  pl.BlockSpec((1, 6, d), lambda bi, i: (bi, 0, 0)), pl.BlockSpec((1, d), const)],
        out_specs=[tile(d), tile(d)],
        out_shape=[jax.ShapeDtypeStruct((b, s, d), F32), jax.ShapeDtypeStruct((b, s, d), BF16)],
        compiler_params=_cparams(("parallel", "parallel")),
        name="outproj",
    )(o, w_out, x, mod3, g2)


def _topk_sublanes(s, flat, k):
    t = s.shape[1]
    big = jnp.iinfo(jnp.int32).max
    rowk = lax.broadcasted_iota(I32, (k, t), 0)
    vals = jnp.zeros((k, t), F32)
    labs = jnp.zeros((k, t), I32)
    for r in range(k):
        m = jnp.max(s, axis=0, keepdims=True)
        lab = jnp.min(jnp.where(s == m, flat, big), axis=0, keepdims=True)
        vals = jnp.where(rowk == r, m, vals)
        labs = jnp.where(rowk == r, lab, labs)
        s = jnp.where(flat == lab, -jnp.inf, s)
    return vals, labs


def _select_head(h, qt_ref, sk_ref, gates_t_ref, idst_ref):
    ts = qt_ref.shape[1]
    k = PEER_TOPK
    key_iota = lax.broadcasted_iota(I32, (PEER_N_KEYS, ts), 0)
    sub8 = lax.broadcasted_iota(I32, (8, ts), 0)
    flat_rows = [lax.broadcasted_iota(I32, (k, ts), 0)] + [i * k + sub8 for i in range(1, k)]
    cand_flat = jnp.concatenate(flat_rows, axis=0)
    r0 = pl.multiple_of(h * 2 * PEER_HALF, 2 * PEER_HALF)
    tops = []
    for p in range(2):
        q = qt_ref[pl.ds(r0 + p * PEER_HALF, PEER_HALF), :]
        s = jnp.dot(sk_ref[2 * h + p], q, preferred_element_type=F32)
        tops.append(_topk_sublanes(s, key_iota, k))
    (v1, i1), (v2, i2) = tops
    vals = [v1[0:1] + v2] + [v1[i:i + 1] + v2[:8] for i in range(1, k)]
    eids = [i1[0:1] * PEER_N_KEYS + i2] + [i1[i:i + 1] * PEER_N_KEYS + i2[:8] for i in range(1, k)]
    cand = jnp.concatenate(vals, axis=0)
    cid = jnp.concatenate(eids, axis=0)
    best, pos = _topk_sublanes(cand, cand_flat, k)
    rowk = lax.broadcasted_iota(I32, (k, ts), 0)
    ids = jnp.zeros((k, ts), I32)
    for r in range(k):
        pick = jnp.sum(jnp.where(cand_flat == pos[r:r + 1], cid, 0), axis=0, keepdims=True)
        ids = jnp.where(rowk == r, pick, ids)
    e = jnp.exp(best - best[0:1])
    o0 = pl.multiple_of(h * k, k)
    gates_t_ref[pl.ds(o0, k), :] = e / jnp.sum(e, axis=0, keepdims=True)
    idst_ref[pl.ds(o0, k), :] = ids


def _peer_kernel(h_cur_ref, h_sel_ref, h_s0_ref, h_s1_ref, x1_ref, mod_ref, fg_ref, wqt_ref, sk_ref, uv_hbm, o_ref,
                 buf0, buf1, sem, ids_s, gates_s, qt_ref, idst_ref, idtok_ref, gates_t_ref, mix_ref, sem_s,
                 *, nsteps):
    tt = PEER_TILE
    nsel = PEER_HEADS * PEER_TOPK
    d = h_cur_ref.shape[1]
    npair = PEER_GROUP // 2
    i = pl.program_id(0)
    bufs = (buf0, buf1)
    r_cur, r_next, r_sel = lax.rem(i, 3), lax.rem(i + 1, 3), lax.rem(i + 2, 3)

    def row_copy(ring, tok, t, e, sl):
        return pltpu.make_async_copy(uv_hbm.at[ids_s[ring, tok, e]], bufs[sl].at[pl.ds(t * nsel + e, 1)],
                                     sem.at[sl])

    def issue(ring, tok0, sl):
        for t in range(tt):
            for e in range(nsel):
                row_copy(ring, tok0 + t, t, e, sl).start(priority=e % 2)

    def wait_slot(sl):
        pltpu.make_async_copy(bufs[1 - sl], bufs[sl], sem.at[sl]).wait()

    def mix(sl, k):
        buf = bufs[sl]
        t0 = pl.multiple_of(k * tt, tt)
        hx = h_cur_ref[pl.ds(t0, tt), :].astype(F32)
        lane_t = lax.broadcasted_iota(I32, (nsel, tt), 1)
        a = jnp.zeros((nsel, tt), F32)
        for t in range(tt):
            u = pltpu.bitcast(buf[t * nsel:(t + 1) * nsel, :] << 16, F32)
            prod = u * hx[t:t + 1, :]
            part = prod[:, 0:LANES]
            for c in range(1, d // LANES):
                part = part + prod[:, c * LANES:(c + 1) * LANES]
            a = jnp.where(lane_t == t, jnp.sum(part, axis=1, keepdims=True), a)
        gelu = 0.5 * a * (1.0 + lax.erf(a * (2.0 ** -0.5)))
        w = gates_s[r_cur, k] * gelu
        hi_mask = jnp.uint32(0xFFFF0000)
        for t in range(tt):
            v = pltpu.bitcast(buf[t * nsel:(t + 1) * nsel, :] & hi_mask, F32)
            mix_ref[pl.ds(t0 + t, 1), :] = jnp.sum(v * w[:, t:t + 1], axis=0, keepdims=True)

    def project(h_ref):
        qt_ref[...] = lax.dot_general(wqt_ref[...], h_ref[...], (((1,), (1,)), ((), ())),
                                      preferred_element_type=F32).astype(BF16)

    def publish(ring):
        idtok_ref[...] = idst_ref[...].T
        cp = pltpu.make_async_copy(idtok_ref, ids_s.at[ring], sem_s.at[0])
        cp.start()
        cp.wait()
        for k in range(PEER_GROUP):
            gates_s[ring, k] = gates_t_ref[:, k * tt:(k + 1) * tt]

    def select_all(h_ref, ring):
        project(h_ref)

        def head(h, c):
            _select_head(h, qt_ref, sk_ref, gates_t_ref, idst_ref)
            return c
        lax.fori_loop(0, PEER_HEADS, head, 0)
        publish(ring)

    @pl.when(i == 0)
    def _():
        select_all(h_s0_ref, 0)
        select_all(h_s1_ref, 1)

        def body(t, c):
            for e in range(nsel):
                row_copy(0, t, t, e, 0).start(priority=e % 2)
            return c
        lax.fori_loop(0, tt, body, 0)

    assert PEER_HEADS == 2 * npair

    def pair(p, c):
        k0 = 2 * p
        last = p == npair - 1

        @pl.when(p == 0)
        def _():
            project(h_sel_ref)

        wait_slot(0)
        issue(r_cur, (k0 + 1) * tt, 1)
        mix(0, k0)
        _select_head(2 * p, qt_ref, sk_ref, gates_t_ref, idst_ref)
        wait_slot(1)
        issue(jnp.where(last, r_next, r_cur), jnp.where(last, 0, (k0 + 2) * tt), 0)
        mix(1, k0 + 1)
        _select_head(2 * p + 1, qt_ref, sk_ref, gates_t_ref, idst_ref)

        @pl.when(last)
        def _():
            publish(r_sel)
        return c

    lax.fori_loop(0, npair, pair, 0)

    y = x1_ref[...] + mod_ref[0, 5:6, :] * mix_ref[...]
    ms = jnp.mean(y * y, axis=-1, keepdims=True)
    o_ref[...] = y * lax.rsqrt(ms + NORM_EPS) * fg_ref[...]

    @pl.when(i == nsteps - 1)
    def _():
        wait_slot(0)


def _peer_call(h2, x1, mod3, fg, wqt, sk, uv, seq):
    t, d = h2.shape
    nsel = PEER_HEADS * PEER_TOPK
    tt = PEER_TILE
    rows_per_step = PEER_GROUP * tt
    assert rows_per_step == SEL_TILE and seq % rows_per_step == 0 and PEER_HEADS % (PEER_GROUP // 2) == 0
    nsteps = t // rows_per_step
    assert nsteps >= 2
    per_batch = seq // rows_per_step
    rows = lambda f: pl.BlockSpec((rows_per_step, d), f)
    return pl.pallas_call(
        functools.partial(_peer_kernel, nsteps=nsteps),
        grid=(nsteps,),
        in_specs=[rows(lambda i: (i, 0)),
                  rows(lambda i: (jnp.minimum(i + 2, nsteps - 1), 0)),
                  rows(lambda i: (0, 0)),
                  rows(lambda i: (1, 0)),
                  rows(lambda i: (i, 0)),
                  pl.BlockSpec((1, 6, d), lambda i: (i // per_batch, 0, 0)),
                  pl.BlockSpec((1, d), lambda i: (0, 0)),
                  pl.BlockSpec(wqt.shape, lambda i: (0, 0)),
                  pl.BlockSpec(sk.shape, lambda i: (0, 0, 0)),
                  pl.BlockSpec(memory_space=pl.ANY)],
        out_specs=rows(lambda i: (i, 0)),
        out_shape=jax.ShapeDtypeStruct((t, d), F32),
        scratch_shapes=[pltpu.VMEM((tt * nsel, d), jnp.uint32),
                        pltpu.VMEM((tt * nsel, d), jnp.uint32),
                        pltpu.SemaphoreType.DMA((2,)),
                        pltpu.SMEM((3, rows_per_step, nsel), I32),
                        pltpu.VMEM((3, PEER_GROUP, nsel, tt), F32),
                        pltpu.VMEM((wqt.shape[0], rows_per_step), BF16),
                        pltpu.VMEM((nsel, rows_per_step), I32),
                        pltpu.VMEM((rows_per_step, nsel), I32),
                        pltpu.VMEM((nsel, rows_per_step), F32),
                        pltpu.VMEM((rows_per_step, d), F32),
                        pltpu.SemaphoreType.DMA((1,))],
        compiler_params=_cparams(("arbitrary",)),
        name="peer",
    )(h2, h2, h2, h2, x1, mod3, fg, wqt, sk, uv)


def _rope_tables(seq):
    half = HEAD_DIM // 4
    freqs = ROPE_THETA ** (-jnp.arange(half, dtype=F32) / half)
    pos = jnp.arange(seq, dtype=jnp.int32)
    ang_r = (pos // GRID_W).astype(F32)[:, None] * freqs[None, :]
    ang_c = (pos % GRID_W).astype(F32)[:, None] * freqs[None, :]
    z = jnp.zeros_like(ang_r)
    cos = jnp.concatenate([jnp.cos(ang_r)] * 2 + [jnp.cos(ang_c)] * 2, axis=1)
    sa = jnp.concatenate([-jnp.sin(ang_r), z, -jnp.sin(ang_c), z], axis=1)
    sb = jnp.concatenate([z, jnp.sin(ang_r), z, jnp.sin(ang_c)], axis=1)
    rep = LANES // HEAD_DIM
    return tuple(jnp.tile(a, (1, rep)) for a in (cos, sa, sb))


def kernel(x, c, ctx, c_ctx, w_mod, b_mod, norm1_g, norm2_g, w_in, w_out, diff_lq1, diff_lk1, diff_lq2,
           diff_lk2, diff_subln_g, gqa_q_norm_g, gqa_k_norm_g, peer_wq, peer_subkeys, peer_u, peer_v,
           final_norm_g):
    b, s, d = x.shape
    assert w_mod.shape[0] == 1, "depth-1 block"
    assert s % IN_TILE == 0 and s % Q_TILE == 0 and ctx.shape[1] % LANES == 0

    pad = (-(b + 1)) % 8
    cc = jnp.concatenate([c, c_ctx[None, :], jnp.zeros((pad, d), F32)], axis=0)
    mod3 = _mod_call(cc, w_mod[0], b_mod[0][None, :]).reshape(cc.shape[0], 6, d)

    w0 = w_in[0]
    o_dk, o_dv, o_gq, o_gk, o_gv = (DIFF_WIDTH, 2 * DIFF_WIDTH, 3 * DIFF_WIDTH, 3 * DIFF_WIDTH + GQA_WIDTH,
                                    3 * DIFF_WIDTH + GQA_WIDTH + GQA_KV_WIDTH)
    group = GQA_Q_HEADS // GQA_KV_HEADS
    head_order = [j * group + t for t in range(group) for j in range(GQA_KV_HEADS)]
    gq_cols = jnp.asarray([hd * HEAD_DIM + e for hd in head_order for e in range(HEAD_DIM)], dtype=jnp.int32)
    w_gq = w0[:, o_gq:o_gk][:, gq_cols]
    w_x = jnp.concatenate([w0[:, :o_gq], w_gq, w0[:, o_gk:]], axis=1).astype(BF16)
    w_c = jnp.concatenate([w0[:, o_dk:o_gq], w0[:, o_gk:]], axis=1).astype(BF16)
    w_o = jnp.concatenate([w_out[0][:DIFF_WIDTH], w_out[0][DIFF_WIDTH:][gq_cols]], axis=0).astype(BF16)

    seg = jnp.arange(GQA_WIDTH) // HEAD_DIM
    bd = (seg[:, None] == seg[None, :]).astype(BF16)
    gqg = jnp.tile(gqa_q_norm_g[0], GQA_Q_HEADS)[None, :]
    gkg = jnp.tile(gqa_k_norm_g[0], GQA_KV_HEADS)[None, :]
    g1 = norm1_g[0][None, :]
    tables = _rope_tables(s)

    dq, gq, dkx, gkx, dvx, gvx = _inproj_call(x, mod3, lambda bi, i: (bi, 0, 0), g1, w_x, tables, bd, gqg, gkg,
                                              with_q=True, tile=IN_TILE)
    ctx_tile = ctx.shape[1]
    ctx_tables = tuple(a[:ctx_tile] for a in tables)
    dkc, gkc, dvc, gvc = _inproj_call(ctx, mod3, lambda bi, i: (b, 0, 0), g1, w_c, ctx_tables, bd, gqg, gkg,
                                      with_q=False, tile=ctx_tile)

    lamv = jnp.stack([diff_lq1[0], diff_lk1[0], diff_lq2[0], diff_lk2[0]], axis=0).astype(F32)
    subg = diff_subln_g[0][:, None]
    o = _attn_call(dq, gq, dkx, gkx, dvx, gvx, dkc, gkc, dvc, gvc, lamv, subg)

    x1, h2 = _out_call(o, w_o, x, mod3, norm2_g[0][None, :])

    t = b * s
    h2f = h2.reshape(t, d)
    wqt = peer_wq[0].T.astype(BF16)
    sk = peer_subkeys[0].reshape(PEER_HEADS * 2, PEER_N_KEYS, PEER_HALF).astype(BF16)
    half_bits = lambda a: lax.bitcast_convert_type(a.astype(BF16), jnp.uint16).astype(jnp.uint32)
    uv = (half_bits(peer_u[0]) | (half_bits(peer_v[0]) << 16))[:, None, :]
    out = _peer_call(h2f, x1.reshape(t, d), mod3, final_norm_g[None, :], wqt, sk, uv, s)
    return out.reshape(b, s, d)
```

```python
import functools
import math

import jax
import jax.numpy as jnp
from jax import lax
from jax.experimental import pallas as pl
from jax.experimental.pallas import tpu as pltpu

F32 = jnp.float32
BF16 = jnp.bfloat16
I32 = jnp.int32

HEAD_DIM = 64
GRID_W = 64
ROPE_THETA = 10000.0
NORM_EPS = 1e-6
DIFF_HEADS = 4
DIFF_V_DIM = 2 * HEAD_DIM
DIFF_WIDTH = DIFF_HEADS * DIFF_V_DIM
GQA_Q_HEADS = 8
GQA_KV_HEADS = 2
GQA_WIDTH = GQA_Q_HEADS * HEAD_DIM
GQA_KV_WIDTH = GQA_KV_HEADS * HEAD_DIM
PEER_HEADS = 8
PEER_N_KEYS = 128
PEER_HALF = 128
PEER_TOPK = 16
LAMBDA_INIT = 0.8 - 0.6 * math.exp(-0.3 * 0)
LANES = 128
SUBLANES = 8
VMEM_LIMIT = 56 * 1024 * 1024

IN_TILE = 1024
Q_TILE = 512
OUT_TILE = 256
SEL_TILE = 128
PEER_TILE = 16
PEER_GROUP = SEL_TILE // PEER_TILE
ATTN_GROUP = 8


def _cparams(sem):
    return pltpu.CompilerParams(dimension_semantics=sem, vmem_limit_bytes=VMEM_LIMIT)


def _split_bf16(a):
    hi = a.astype(BF16)
    lo = (a - hi.astype(F32)).astype(BF16)
    return hi, lo


def _mod_kernel(c_ref, w_ref, b_ref, o_ref):
    s = jax.nn.silu(c_ref[...])
    s_hi, s_lo = _split_bf16(s)
    w_hi, w_lo = _split_bf16(w_ref[...])
    acc = jnp.dot(s_hi, w_hi, preferred_element_type=F32)
    acc += jnp.dot(s_hi, w_lo, preferred_element_type=F32)
    acc += jnp.dot(s_lo, w_hi, preferred_element_type=F32)
    o_ref[...] = acc + b_ref[...]


def _mod_call(cc, w_mod, b_mod):
    rows, d = cc.shape
    n = w_mod.shape[1]
    tn = n // 4
    return pl.pallas_call(
        _mod_kernel,
        grid=(n // tn,),
        in_specs=[pl.BlockSpec((rows, d), lambda j: (0, 0)),
                  pl.BlockSpec((d, tn), lambda j: (0, j)),
                  pl.BlockSpec((1, tn), lambda j: (0, j))],
        out_specs=pl.BlockSpec((rows, tn), lambda j: (0, j)),
        out_shape=jax.ShapeDtypeStruct((rows, n), F32),
        compiler_params=_cparams(("arbitrary",)),
        name="mod",
    )(cc, w_mod, b_mod)


def _rope_tile(x, cos, sa, sb):
    return x * cos + pltpu.roll(x, LANES - 16, 1) * sa + pltpu.roll(x, 16, 1) * sb


def _head_rms(x, bd, g):
    sq = x * x
    hi, lo = _split_bf16(sq)
    ssum = jnp.dot(hi, bd, preferred_element_type=F32) + jnp.dot(lo, bd, preferred_element_type=F32)
    return x * lax.rsqrt(ssum * (1.0 / HEAD_DIM) + NORM_EPS) * g


def _inproj_kernel(x_ref, mod_ref, g1_ref, w_ref, cos_ref, sa_ref, sb_ref, bd_ref, gqg_ref, gkg_ref,
                   *out_refs, with_q, with_rope):
    x = x_ref[0]
    ms = jnp.mean(x * x, axis=-1, keepdims=True)
    y = x * lax.rsqrt(ms + NORM_EPS) * g1_ref[...]
    h = (y * (1.0 + mod_ref[0, 1:2, :]) + mod_ref[0, 0:1, :]).astype(BF16)
    p = jnp.dot(h, w_ref[...], preferred_element_type=F32)

    if with_rope:
        cos, sa, sb = cos_ref[...], sa_ref[...], sb_ref[...]
        rope = lambda t: _rope_tile(t, cos, sa, sb)
    else:
        rope = lambda t: t
    bd = bd_ref[...]
    scale = HEAD_DIM ** -0.5

    if with_q:
        dq_ref, gq_ref, dk_ref, gk_ref, dvt_ref, gvt_ref = out_refs
        off = 0
        for t in range(DIFF_WIDTH // LANES):
            sl = slice(t * LANES, (t + 1) * LANES)
            dq_ref[0, :, sl] = (rope(p[:, off + t * LANES: off + (t + 1) * LANES]) * scale).astype(BF16)
        off += DIFF_WIDTH
    else:
        dk_ref, gk_ref, dvt_ref, gvt_ref = out_refs
        off = 0
    for t in range(DIFF_WIDTH // LANES):
        sl = slice(t * LANES, (t + 1) * LANES)
        dk_ref[0, :, sl] = rope(p[:, off + t * LANES: off + (t + 1) * LANES]).astype(BF16)
    off += DIFF_WIDTH
    dvt_ref[0, 0] = p[:, off: off + DIFF_WIDTH].T.astype(BF16)
    off += DIFF_WIDTH
    if with_q:
        gqn = _head_rms(p[:, off: off + GQA_WIDTH], bd, gqg_ref[...])
        for t in range(GQA_WIDTH // LANES):
            sl = slice(t * LANES, (t + 1) * LANES)
            gq_ref[0, :, sl] = (rope(gqn[:, sl]) * scale).astype(BF16)
        off += GQA_WIDTH
    gkn = _head_rms(p[:, off: off + GQA_KV_WIDTH], bd[:GQA_KV_WIDTH, :GQA_KV_WIDTH], gkg_ref[...])
    gk_ref[0] = rope(gkn).astype(BF16)
    off += GQA_KV_WIDTH
    gvt_ref[0, 0] = p[:, off: off + GQA_KV_WIDTH].T.astype(BF16)


def _inproj_call(x, mod3, mod_row, g1, w, tables, bd, gqg, gkg, *, with_q, tile):
    b, s, d = x.shape
    n = w.shape[1]
    nt = s // tile
    cos, sa, sb = tables
    kv_shapes = [jax.ShapeDtypeStruct((b, s, DIFF_WIDTH), BF16),
                 jax.ShapeDtypeStruct((b, s, GQA_KV_WIDTH), BF16),
                 jax.ShapeDtypeStruct((b, nt, DIFF_WIDTH, tile), BF16),
                 jax.ShapeDtypeStruct((b, nt, GQA_KV_WIDTH, tile), BF16)]
    kv_specs = [pl.BlockSpec((1, tile, DIFF_WIDTH), lambda bi, i: (bi, i, 0)),
                pl.BlockSpec((1, tile, GQA_KV_WIDTH), lambda bi, i: (bi, i, 0)),
                pl.BlockSpec((1, 1, DIFF_WIDTH, tile), lambda bi, i: (bi, i, 0, 0)),
                pl.BlockSpec((1, 1, GQA_KV_WIDTH, tile), lambda bi, i: (bi, i, 0, 0))]
    if with_q:
        out_shapes = [jax.ShapeDtypeStruct((b, s, DIFF_WIDTH), BF16),
                      jax.ShapeDtypeStruct((b, s, GQA_WIDTH), BF16)] + kv_shapes
        out_specs = [pl.BlockSpec((1, tile, DIFF_WIDTH), lambda bi, i: (bi, i, 0)),
                     pl.BlockSpec((1, tile, GQA_WIDTH), lambda bi, i: (bi, i, 0))] + kv_specs
    else:
        out_shapes, out_specs = kv_shapes, kv_specs
    const = lambda bi, i: (0, 0)
    return pl.pallas_call(
        functools.partial(_inproj_kernel, with_q=with_q, with_rope=with_q),
        grid=(b, nt),
        in_specs=[pl.BlockSpec((1, tile, d), lambda bi, i: (bi, i, 0)),
                  pl.BlockSpec((1, 6, d), mod_row),
                  pl.BlockSpec((1, d), const),
                  pl.BlockSpec((d, n), const),
                  pl.BlockSpec((tile, LANES), lambda bi, i: (i, 0)),
                  pl.BlockSpec((tile, LANES), lambda bi, i: (i, 0)),
                  pl.BlockSpec((tile, LANES), lambda bi, i: (i, 0)),
                  pl.BlockSpec(bd.shape, const),
                  pl.BlockSpec(gqg.shape, const),
                  pl.BlockSpec(gkg.shape, const)],
        out_specs=out_specs,
        out_shape=out_shapes,
        compiler_params=_cparams(("parallel", "parallel")),
        name="inproj_x" if with_q else "inproj_ctx",
    )(x, mod3, g1, w, cos, sa, sb, bd, gqg, gkg)


def _attn_steps(ks, qs, vts, carries):
    ss = [lax.dot_general(k, q, (((1,), (1,)), ((), ())), preferred_element_type=F32) for k, q in zip(ks, qs)]
    stats = []
    for s, (m, l, _) in zip(ss, carries):
        m_new = jnp.maximum(m, jnp.max(s, axis=0, keepdims=True))
        alpha = jnp.exp(m - m_new)
        p = jnp.exp(s - m_new)
        stats.append((m_new, alpha, alpha * l + jnp.sum(p, axis=0, keepdims=True), p.astype(BF16)))
    return tuple((m_new, l, alpha * acc + jnp.dot(vt, p, preferred_element_type=F32))
                 for (m_new, alpha, l, p), vt, (_, _, acc) in zip(stats, vts, carries))


def _attn_kernel(dq_ref, gq_ref, dkx_ref, gkx_ref, dvx_ref, gvx_ref, dkc_ref, gkc_ref, dvc_ref, gvc_ref,
                 lamv_ref, subg_ref, o_ref):
    tq = dq_ref.shape[1]
    n_chunks, chunk = dvx_ref.shape[1], dvx_ref.shape[3]
    lv = lamv_ref[...]
    lam = (jnp.exp(jnp.sum(lv[0:1] * lv[1:2], axis=-1, keepdims=True))
           - jnp.exp(jnp.sum(lv[2:3] * lv[3:4], axis=-1, keepdims=True)) + LAMBDA_INIT)
    low = lax.broadcasted_iota(I32, (tq, LANES), 1) < HEAD_DIM

    def run_units(units):
        def go(kx_ref, kc_ref, vx_ref, vc_ref, dv):
            qs = [q for q, _, _ in units]
            init = tuple((jnp.full((1, tq), -jnp.inf, F32), jnp.zeros((1, tq), F32), jnp.zeros((dv, tq), F32))
                         for _ in units)
            carries = _attn_steps([kc_ref[0, :, col:col + LANES] for _, col, _ in units], qs,
                                  [vc_ref[0, 0, r0:r0 + dv, :] for _, _, r0 in units], init)

            def body(c, carries):
                off = pl.multiple_of(c * chunk, chunk)
                return _attn_steps([kx_ref[0, pl.ds(off, chunk), col:col + LANES] for _, col, _ in units], qs,
                                   [vx_ref[0, c, r0:r0 + dv, :] for _, _, r0 in units], carries)

            return [acc / l for _, l, acc in lax.fori_loop(0, n_chunks, body, carries)]
        return go

    zero = jnp.zeros((tq, LANES), BF16)
    for h0 in range(0, DIFF_HEADS, ATTN_GROUP // 2):
        heads = range(h0, h0 + ATTN_GROUP // 2)
        units = []
        for h in heads:
            qh = dq_ref[0, :, h * LANES:(h + 1) * LANES]
            units += [(jnp.where(low, qh, zero), h * LANES, h * DIFF_V_DIM),
                      (jnp.where(low, zero, qh), h * LANES, h * DIFF_V_DIM)]
        outs = run_units(units)(dkx_ref, dkc_ref, dvx_ref, dvc_ref, DIFF_V_DIM)
        for n, h in enumerate(heads):
            od = outs[2 * n] - lam * outs[2 * n + 1]
            ms = jnp.mean(od * od, axis=0, keepdims=True)
            od = od * lax.rsqrt(ms + NORM_EPS) * subg_ref[...] * (1.0 - LAMBDA_INIT)
            o_ref[0, :, h * LANES:(h + 1) * LANES] = od.T.astype(BF16)
    for t0 in range(0, GQA_Q_HEADS // 2, ATTN_GROUP // 2):
        tiles = range(t0, t0 + ATTN_GROUP // 2)
        units = []
        for t in tiles:
            qt = gq_ref[0, :, t * LANES:(t + 1) * LANES]
            units += [(jnp.where(low, qt, zero), 0, 0), (jnp.where(low, zero, qt), 0, HEAD_DIM)]
        outs = run_units(units)(gkx_ref, gkc_ref, gvx_ref, gvc_ref, HEAD_DIM)
        for n, t in enumerate(tiles):
            og = jnp.concatenate([outs[2 * n], outs[2 * n + 1]], axis=0)
            o_ref[0, :, DIFF_WIDTH + t * LANES: DIFF_WIDTH + (t + 1) * LANES] = og.T.astype(BF16)


def _attn_call(dq, gq, dkx, gkx, dvx, gvx, dkc, gkc, dvc, gvc, lamv, subg):
    b, s, _ = dq.shape
    nq = s // Q_TILE
    full3 = lambda a: pl.BlockSpec((1,) + a.shape[1:], lambda bi, i: (bi, 0, 0))
    full4 = lambda a: pl.BlockSpec((1,) + a.shape[1:], lambda bi, i: (bi, 0, 0, 0))
    const = lambda bi, i: (0, 0)
    return pl.pallas_call(
        _attn_kernel,
        grid=(b, nq),
        in_specs=[pl.BlockSpec((1, Q_TILE, DIFF_WIDTH), lambda bi, i: (bi, i, 0)),
                  pl.BlockSpec((1, Q_TILE, GQA_WIDTH), lambda bi, i: (bi, i, 0)),
                  full3(dkx), full3(gkx), full4(dvx), full4(gvx),
                  full3(dkc), full3(gkc), full4(dvc), full4(gvc),
                  pl.BlockSpec(lamv.shape, const), pl.BlockSpec(subg.shape, const)],
        out_specs=pl.BlockSpec((1, Q_TILE, DIFF_WIDTH + GQA_WIDTH), lambda bi, i: (bi, i, 0)),
        out_shape=jax.ShapeDtypeStruct((b, s, DIFF_WIDTH + GQA_WIDTH), BF16),
        compiler_params=_cparams(("parallel", "arbitrary")),
        name="attn",
    )(dq, gq, dkx, gkx, dvx, gvx, dkc, gkc, dvc, gvc, lamv, subg)


def _out_kernel(o_ref, w_ref, x_ref, mod_ref, g2_ref, x1_ref, h2_ref):
    attn = jnp.dot(o_ref[0], w_ref[...], preferred_element_type=F32)
    x1 = x_ref[0] + mod_ref[0, 2:3, :] * attn
    x1_ref[0] = x1
    ms = jnp.mean(x1 * x1, axis=-1, keepdims=True)
    y = x1 * lax.rsqrt(ms + NORM_EPS) * g2_ref[...]
    h2_ref[0] = (y * (1.0 + mod_ref[0, 4:5, :]) + mod_ref[0, 3:4, :]).astype(BF16)


def _out_call(o, w_out, x, mod3, g2):
    b, s, d = x.shape
    const = lambda bi, i: (0, 0)
    tile = lambda w: pl.BlockSpec((1, OUT_TILE, w), lambda bi, i: (bi, i, 0))
    return pl.pallas_call(
        _out_kernel,
        grid=(b, s // OUT_TILE),
        in_specs=[tile(o.shape[2]), pl.BlockSpec(w_out.shape, const), tile(d),
                  pl.BlockSpec((1, 6, d), lambda bi, i: (bi, 0, 0)), pl.BlockSpec((1, d), const)],
        out_specs=[tile(d), tile(d)],
        out_shape=[jax.ShapeDtypeStruct((b, s, d), F32), jax.ShapeDtypeStruct((b, s, d), BF16)],
        compiler_params=_cparams(("parallel", "parallel")),
        name="outproj",
    )(o, w_out, x, mod3, g2)


def _topk_sublanes(s, flat, k):
    t = s.shape[1]
    big = jnp.iinfo(jnp.int32).max
    rowk = lax.broadcasted_iota(I32, (k, t), 0)
    vals = jnp.zeros((k, t), F32)
    labs = jnp.zeros((k, t), I32)
    for r in range(k):
        m = jnp.max(s, axis=0, keepdims=True)
        lab = jnp.min(jnp.where(s == m, flat, big), axis=0, keepdims=True)
        vals = jnp.where(rowk == r, m, vals)
        labs = jnp.where(rowk == r, lab, labs)
        s = jnp.where(flat == lab, -jnp.inf, s)
    return vals, labs


def _select_head(h, qt_ref, sk_ref, gates_t_ref, idst_ref):
    ts = qt_ref.shape[1]
    k = PEER_TOPK
    key_iota = lax.broadcasted_iota(I32, (PEER_N_KEYS, ts), 0)
    sub8 = lax.broadcasted_iota(I32, (8, ts), 0)
    flat_rows = [lax.broadcasted_iota(I32, (k, ts), 0)] + [i * k + sub8 for i in range(1, k)]
    cand_flat = jnp.concatenate(flat_rows, axis=0)
    r0 = pl.multiple_of(h * 2 * PEER_HALF, 2 * PEER_HALF)
    tops = []
    for p in range(2):
        q = qt_ref[pl.ds(r0 + p * PEER_HALF, PEER_HALF), :]
        s = jnp.dot(sk_ref[2 * h + p], q, preferred_element_type=F32)
        tops.append(_topk_sublanes(s, key_iota, k))
    (v1, i1), (v2, i2) = tops
    vals = [v1[0:1] + v2] + [v1[i:i + 1] + v2[:8] for i in range(1, k)]
    eids = [i1[0:1] * PEER_N_KEYS + i2] + [i1[i:i + 1] * PEER_N_KEYS + i2[:8] for i in range(1, k)]
    cand = jnp.concatenate(vals, axis=0)
    cid = jnp.concatenate(eids, axis=0)
    best, pos = _topk_sublanes(cand, cand_flat, k)
    rowk = lax.broadcasted_iota(I32, (k, ts), 0)
    ids = jnp.zeros((k, ts), I32)
    for r in range(k):
        pick = jnp.sum(jnp.where(cand_flat == pos[r:r + 1], cid, 0), axis=0, keepdims=True)
        ids = jnp.where(rowk == r, pick, ids)
    e = jnp.exp(best - best[0:1])
    o0 = pl.multiple_of(h * k, k)
    gates_t_ref[pl.ds(o0, k), :] = e / jnp.sum(e, axis=0, keepdims=True)
    idst_ref[pl.ds(o0, k), :] = ids


def _peer_kernel(h_cur_ref, h_sel_ref, h_s0_ref, h_s1_ref, x1_ref, mod_ref, fg_ref, wqt_ref, sk_ref, uv_hbm, o_ref,
                 buf0, buf1, sem, ids_s, gates_s, qt_ref, idst_ref, idtok_ref, gates_t_ref, mix_ref, sem_s,
                 *, nsteps):
    tt = PEER_TILE
    nsel = PEER_HEADS * PEER_TOPK
    d = h_cur_ref.shape[1]
    npair = PEER_GROUP // 2
    i = pl.program_id(0)
    bufs = (buf0, buf1)
    r_cur, r_next, r_sel = lax.rem(i, 3), lax.rem(i + 1, 3), lax.rem(i + 2, 3)

    def row_copy(ring, tok, t, e, sl):
        return pltpu.make_async_copy(uv_hbm.at[ids_s[ring, tok, e]],
                                     bufs[sl].at[pl.ds((t * nsel + e) * SUBLANES, SUBLANES)], sem.at[sl])

    def issue(ring, tok0, sl):
        for t in range(tt):
            for e in range(nsel):
                row_copy(ring, tok0 + t, t, e, sl).start(priority=e % 2)

    def wait_slot(sl):
        pltpu.make_async_copy(bufs[1 - sl], bufs[sl], sem.at[sl]).wait()

    def mix(sl, k):
        buf = bufs[sl]
        t0 = pl.multiple_of(k * tt, tt)
        hx = h_cur_ref[pl.ds(t0, tt), :].astype(F32)
        lane_t = lax.broadcasted_iota(I32, (nsel, tt), 1)

        def rows(t):
            base = t * nsel * SUBLANES
            return jnp.concatenate([buf[pl.ds(base + c, nsel, stride=SUBLANES), :] for c in range(SUBLANES)],
                                   axis=1)

        a = jnp.zeros((nsel, tt), F32)
        for t in range(tt):
            u = pltpu.bitcast(rows(t) << 16, F32)
            prod = u * hx[t:t + 1, :]
            part = prod[:, 0:LANES]
            for c in range(1, d // LANES):
                part = part + prod[:, c * LANES:(c + 1) * LANES]
            a = jnp.where(lane_t == t, jnp.sum(part, axis=1, keepdims=True), a)
        gelu = 0.5 * a * (1.0 + lax.erf(a * (2.0 ** -0.5)))
        w = gates_s[r_cur, k] * gelu
        hi_mask = jnp.uint32(0xFFFF0000)
        for t in range(tt):
            v = pltpu.bitcast(rows(t) & hi_mask, F32)
            mix_ref[pl.ds(t0 + t, 1), :] = jnp.sum(v * w[:, t:t + 1], axis=0, keepdims=True)

    def project(h_ref):
        qt_ref[...] = lax.dot_general(wqt_ref[...], h_ref[...], (((1,), (1,)), ((), ())),
                                      preferred_element_type=F32).astype(BF16)

    def publish(ring):
        idtok_ref[...] = idst_ref[...].T
        cp = pltpu.make_async_copy(idtok_ref, ids_s.at[ring], sem_s.at[0])
        cp.start()
        cp.wait()
        for k in range(PEER_GROUP):
            gates_s[ring, k] = gates_t_ref[:, k * tt:(k + 1) * tt]

    def select_all(h_ref, ring):
        project(h_ref)

        def head(h, c):
            _select_head(h, qt_ref, sk_ref, gates_t_ref, idst_ref)
            return c
        lax.fori_loop(0, PEER_HEADS, head, 0)
        publish(ring)

    @pl.when(i == 0)
    def _():
        select_all(h_s0_ref, 0)
        select_all(h_s1_ref, 1)

        def body(t, c):
            for e in range(nsel):
                row_copy(0, t, t, e, 0).start(priority=e % 2)
            return c
        lax.fori_loop(0, tt, body, 0)

    assert PEER_HEADS == 2 * npair

    def pair(p, c):
        k0 = 2 * p
        last = p == npair - 1

        @pl.when(p == 0)
        def _():
            project(h_sel_ref)

        wait_slot(0)
        issue(r_cur, (k0 + 1) * tt, 1)
        mix(0, k0)
        _select_head(2 * p, qt_ref, sk_ref, gates_t_ref, idst_ref)
        wait_slot(1)
        issue(jnp.where(last, r_next, r_cur), jnp.where(last, 0, (k0 + 2) * tt), 0)
        mix(1, k0 + 1)
        _select_head(2 * p + 1, qt_ref, sk_ref, gates_t_ref, idst_ref)

        @pl.when(last)
        def _():
            publish(r_sel)
        return c

    lax.fori_loop(0, npair, pair, 0)

    y = x1_ref[...] + mod_ref[0, 5:6, :] * mix_ref[...]
    ms = jnp.mean(y * y, axis=-1, keepdims=True)
    o_ref[...] = y * lax.rsqrt(ms + NORM_EPS) * fg_ref[...]

    @pl.when(i == nsteps - 1)
    def _():
        wait_slot(0)


def _peer_call(h2, x1, mod3, fg, wqt, sk, uv, seq):
    t, d = h2.shape
    nsel = PEER_HEADS * PEER_TOPK
    tt = PEER_TILE
    rows_per_step = PEER_GROUP * tt
    assert rows_per_step == SEL_TILE and seq % rows_per_step == 0 and PEER_HEADS % (PEER_GROUP // 2) == 0
    nsteps = t // rows_per_step
    assert nsteps >= 2
    per_batch = seq // rows_per_step
    rows = lambda f: pl.BlockSpec((rows_per_step, d), f)
    return pl.pallas_call(
        functools.partial(_peer_kernel, nsteps=nsteps),
        grid=(nsteps,),
        in_specs=[rows(lambda i: (i, 0)),
                  rows(lambda i: (jnp.minimum(i + 2, nsteps - 1), 0)),
                  rows(lambda i: (0, 0)),
                  rows(lambda i: (1, 0)),
                  rows(lambda i: (i, 0)),
                  pl.BlockSpec((1, 6, d), lambda i: (i // per_batch, 0, 0)),
                  pl.BlockSpec((1, d), lambda i: (0, 0)),
                  pl.BlockSpec(wqt.shape, lambda i: (0, 0)),
                  pl.BlockSpec(sk.shape, lambda i: (0, 0, 0)),
                  pl.BlockSpec(memory_space=pl.ANY)],
        out_specs=rows(lambda i: (i, 0)),
        out_shape=jax.ShapeDtypeStruct((t, d), F32),
        scratch_shapes=[pltpu.VMEM((tt * nsel * SUBLANES, LANES), jnp.uint32),
                        pltpu.VMEM((tt * nsel * SUBLANES, LANES), jnp.uint32),
                        pltpu.SemaphoreType.DMA((2,)),
                        pltpu.SMEM((3, rows_per_step, nsel), I32),
                        pltpu.VMEM((3, PEER_GROUP, nsel, tt), F32),
                        pltpu.VMEM((wqt.shape[0], rows_per_step), BF16),
                        pltpu.VMEM((nsel, rows_per_step), I32),
                        pltpu.VMEM((rows_per_step, nsel), I32),
                        pltpu.VMEM((nsel, rows_per_step), F32),
                        pltpu.VMEM((rows_per_step, d), F32),
                        pltpu.SemaphoreType.DMA((1,))],
        compiler_params=_cparams(("arbitrary",)),
        name="peer",
    )(h2, h2, h2, h2, x1, mod3, fg, wqt, sk, uv)


def _rope_tables(seq):
    half = HEAD_DIM // 4
    freqs = ROPE_THETA ** (-jnp.arange(half, dtype=F32) / half)
    pos = jnp.arange(seq, dtype=jnp.int32)
    ang_r = (pos // GRID_W).astype(F32)[:, None] * freqs[None, :]
    ang_c = (pos % GRID_W).astype(F32)[:, None] * freqs[None, :]
    z = jnp.zeros_like(ang_r)
    cos = jnp.concatenate([jnp.cos(ang_r)] * 2 + [jnp.cos(ang_c)] * 2, axis=1)
    sa = jnp.concatenate([-jnp.sin(ang_r), z, -jnp.sin(ang_c), z], axis=1)
    sb = jnp.concatenate([z, jnp.sin(ang_r), z, jnp.sin(ang_c)], axis=1)
    rep = LANES // HEAD_DIM
    return tuple(jnp.tile(a, (1, rep)) for a in (cos, sa, sb))


def kernel(x, c, ctx, c_ctx, w_mod, b_mod, norm1_g, norm2_g, w_in, w_out, diff_lq1, diff_lk1, diff_lq2,
           diff_lk2, diff_subln_g, gqa_q_norm_g, gqa_k_norm_g, peer_wq, peer_subkeys, peer_u, peer_v,
           final_norm_g):
    b, s, d = x.shape
    assert w_mod.shape[0] == 1, "depth-1 block"
    assert s % IN_TILE == 0 and s % Q_TILE == 0 and ctx.shape[1] % LANES == 0

    pad = (-(b + 1)) % 8
    cc = jnp.concatenate([c, c_ctx[None, :], jnp.zeros((pad, d), F32)], axis=0)
    mod3 = _mod_call(cc, w_mod[0], b_mod[0][None, :]).reshape(cc.shape[0], 6, d)

    w0 = w_in[0]
    o_dk, o_dv, o_gq, o_gk, o_gv = (DIFF_WIDTH, 2 * DIFF_WIDTH, 3 * DIFF_WIDTH, 3 * DIFF_WIDTH + GQA_WIDTH,
                                    3 * DIFF_WIDTH + GQA_WIDTH + GQA_KV_WIDTH)
    group = GQA_Q_HEADS // GQA_KV_HEADS
    head_order = [j * group + t for t in range(group) for j in range(GQA_KV_HEADS)]
    gq_cols = jnp.asarray([hd * HEAD_DIM + e for hd in head_order for e in range(HEAD_DIM)], dtype=jnp.int32)
    w_gq = w0[:, o_gq:o_gk][:, gq_cols]
    w_x = jnp.concatenate([w0[:, :o_gq], w_gq, w0[:, o_gk:]], axis=1).astype(BF16)
    w_c = jnp.concatenate([w0[:, o_dk:o_gq], w0[:, o_gk:]], axis=1).astype(BF16)
    w_o = jnp.concatenate([w_out[0][:DIFF_WIDTH], w_out[0][DIFF_WIDTH:][gq_cols]], axis=0).astype(BF16)

    seg = jnp.arange(GQA_WIDTH) // HEAD_DIM
    bd = (seg[:, None] == seg[None, :]).astype(BF16)
    gqg = jnp.tile(gqa_q_norm_g[0], GQA_Q_HEADS)[None, :]
    gkg = jnp.tile(gqa_k_norm_g[0], GQA_KV_HEADS)[None, :]
    g1 = norm1_g[0][None, :]
    tables = _rope_tables(s)

    dq, gq, dkx, gkx, dvx, gvx = _inproj_call(x, mod3, lambda bi, i: (bi, 0, 0), g1, w_x, tables, bd, gqg, gkg,
                                              with_q=True, tile=IN_TILE)
    ctx_tile = ctx.shape[1]
    ctx_tables = tuple(a[:ctx_tile] for a in tables)
    dkc, gkc, dvc, gvc = _inproj_call(ctx, mod3, lambda bi, i: (b, 0, 0), g1, w_c, ctx_tables, bd, gqg, gkg,
                                      with_q=False, tile=ctx_tile)

    lamv = jnp.stack([diff_lq1[0], diff_lk1[0], diff_lq2[0], diff_lk2[0]], axis=0).astype(F32)
    subg = diff_subln_g[0][:, None]
    o = _attn_call(dq, gq, dkx, gkx, dvx, gvx, dkc, gkc, dvc, gvc, lamv, subg)

    x1, h2 = _out_call(o, w_o, x, mod3, norm2_g[0][None, :])

    t = b * s
    h2f = h2.reshape(t, d)
    wqt = peer_wq[0].T.astype(BF16)
    sk = peer_subkeys[0].reshape(PEER_HEADS * 2, PEER_N_KEYS, PEER_HALF).astype(BF16)
    half_bits = lambda a: lax.bitcast_convert_type(a.astype(BF16), jnp.uint16).astype(jnp.uint32)
    uv = (half_bits(peer_u[0]) | (half_bits(peer_v[0]) << 16)).reshape(-1, SUBLANES, LANES)
    assert d == SUBLANES * LANES
    out = _peer_call(h2f, x1.reshape(t, d), mod3, final_norm_g[None, :], wqt, sk, uv, s)
    return out.reshape(b, s, d)
```

```python
import functools
import math

import jax
import jax.numpy as jnp
from jax import lax
from jax.experimental import pallas as pl
from jax.experimental.pallas import tpu as pltpu

F32 = jnp.float32
BF16 = jnp.bfloat16
I32 = jnp.int32

HEAD_DIM = 64
GRID_W = 64
ROPE_THETA = 10000.0
NORM_EPS = 1e-6
DIFF_HEADS = 4
DIFF_V_DIM = 2 * HEAD_DIM
DIFF_WIDTH = DIFF_HEADS * DIFF_V_DIM
GQA_Q_HEADS = 8
GQA_KV_HEADS = 2
GQA_WIDTH = GQA_Q_HEADS * HEAD_DIM
GQA_KV_WIDTH = GQA_KV_HEADS * HEAD_DIM
PEER_HEADS = 8
PEER_N_KEYS = 128
PEER_HALF = 128
PEER_TOPK = 16
LAMBDA_INIT = 0.8 - 0.6 * math.exp(-0.3 * 0)
LANES = 128
SUBLANES = 8
HALF_BITS = 16
HIGH_HALF = 0xFFFF0000
VMEM_LIMIT = 56 * 1024 * 1024

IN_TILE = 1024
Q_TILE = 512
OUT_TILE = 256
SEL_TILE = 128
PEER_TILE = 32
PEER_GROUP = SEL_TILE // PEER_TILE
ATTN_GROUP = 8


def _cparams(sem):
    return pltpu.CompilerParams(dimension_semantics=sem, vmem_limit_bytes=VMEM_LIMIT)


def _split_bf16(a):
    hi = a.astype(BF16)
    lo = (a - hi.astype(F32)).astype(BF16)
    return hi, lo


def _mod_kernel(c_ref, w_ref, b_ref, o_ref):
    s = jax.nn.silu(c_ref[...])
    s_hi, s_lo = _split_bf16(s)
    w_hi, w_lo = _split_bf16(w_ref[...])
    acc = jnp.dot(s_hi, w_hi, preferred_element_type=F32)
    acc += jnp.dot(s_hi, w_lo, preferred_element_type=F32)
    acc += jnp.dot(s_lo, w_hi, preferred_element_type=F32)
    o_ref[...] = acc + b_ref[...]


def _mod_call(cc, w_mod, b_mod):
    rows, d = cc.shape
    n = w_mod.shape[1]
    tn = n // 4
    return pl.pallas_call(
        _mod_kernel,
        grid=(n // tn,),
        in_specs=[pl.BlockSpec((rows, d), lambda j: (0, 0)),
                  pl.BlockSpec((d, tn), lambda j: (0, j)),
                  pl.BlockSpec((1, tn), lambda j: (0, j))],
        out_specs=pl.BlockSpec((rows, tn), lambda j: (0, j)),
        out_shape=jax.ShapeDtypeStruct((rows, n), F32),
        compiler_params=_cparams(("arbitrary",)),
        name="mod",
    )(cc, w_mod, b_mod)


def _rope_tile(x, cos, sa, sb):
    return x * cos + pltpu.roll(x, LANES - 16, 1) * sa + pltpu.roll(x, 16, 1) * sb


def _head_rms(x, bd, g):
    sq = x * x
    hi, lo = _split_bf16(sq)
    ssum = jnp.dot(hi, bd, preferred_element_type=F32) + jnp.dot(lo, bd, preferred_element_type=F32)
    return x * lax.rsqrt(ssum * (1.0 / HEAD_DIM) + NORM_EPS) * g


def _inproj_kernel(x_ref, mod_ref, g1_ref, w_ref, cos_ref, sa_ref, sb_ref, bd_ref, gqg_ref, gkg_ref,
                   *out_refs, with_q, with_rope):
    x = x_ref[0]
    ms = jnp.mean(x * x, axis=-1, keepdims=True)
    y = x * lax.rsqrt(ms + NORM_EPS) * g1_ref[...]
    h = (y * (1.0 + mod_ref[0, 1:2, :]) + mod_ref[0, 0:1, :]).astype(BF16)
    p = jnp.dot(h, w_ref[...], preferred_element_type=F32)

    if with_rope:
        cos, sa, sb = cos_ref[...], sa_ref[...], sb_ref[...]
        rope = lambda t: _rope_tile(t, cos, sa, sb)
    else:
        rope = lambda t: t
    bd = bd_ref[...]
    scale = HEAD_DIM ** -0.5

    if with_q:
        dq_ref, gq_ref, dk_ref, gk_ref, dvt_ref, gvt_ref = out_refs
        off = 0
        for t in range(DIFF_WIDTH // LANES):
            sl = slice(t * LANES, (t + 1) * LANES)
            dq_ref[0, :, sl] = (rope(p[:, off + t * LANES: off + (t + 1) * LANES]) * scale).astype(BF16)
        off += DIFF_WIDTH
    else:
        dk_ref, gk_ref, dvt_ref, gvt_ref = out_refs
        off = 0
    for t in range(DIFF_WIDTH // LANES):
        sl = slice(t * LANES, (t + 1) * LANES)
        dk_ref[0, :, sl] = rope(p[:, off + t * LANES: off + (t + 1) * LANES]).astype(BF16)
    off += DIFF_WIDTH
    dvt_ref[0, 0] = p[:, off: off + DIFF_WIDTH].T.astype(BF16)
    off += DIFF_WIDTH
    if with_q:
        gqn = _head_rms(p[:, off: off + GQA_WIDTH], bd, gqg_ref[...])
        for t in range(GQA_WIDTH // LANES):
            sl = slice(t * LANES, (t + 1) * LANES)
            gq_ref[0, :, sl] = (rope(gqn[:, sl]) * scale).astype(BF16)
        off += GQA_WIDTH
    gkn = _head_rms(p[:, off: off + GQA_KV_WIDTH], bd[:GQA_KV_WIDTH, :GQA_KV_WIDTH], gkg_ref[...])
    gk_ref[0] = rope(gkn).astype(BF16)
    off += GQA_KV_WIDTH
    gvt_ref[0, 0] = p[:, off: off + GQA_KV_WIDTH].T.astype(BF16)


def _inproj_call(x, mod3, mod_row, g1, w, tables, bd, gqg, gkg, *, with_q, tile):
    b, s, d = x.shape
    n = w.shape[1]
    nt = s // tile
    cos, sa, sb = tables
    kv_shapes = [jax.ShapeDtypeStruct((b, s, DIFF_WIDTH), BF16),
                 jax.ShapeDtypeStruct((b, s, GQA_KV_WIDTH), BF16),
                 jax.ShapeDtypeStruct((b, nt, DIFF_WIDTH, tile), BF16),
                 jax.ShapeDtypeStruct((b, nt, GQA_KV_WIDTH, tile), BF16)]
    kv_specs = [pl.BlockSpec((1, tile, DIFF_WIDTH), lambda bi, i: (bi, i, 0)),
                pl.BlockSpec((1, tile, GQA_KV_WIDTH), lambda bi, i: (bi, i, 0)),
                pl.BlockSpec((1, 1, DIFF_WIDTH, tile), lambda bi, i: (bi, i, 0, 0)),
                pl.BlockSpec((1, 1, GQA_KV_WIDTH, tile), lambda bi, i: (bi, i, 0, 0))]
    if with_q:
        out_shapes = [jax.ShapeDtypeStruct((b, s, DIFF_WIDTH), BF16),
                      jax.ShapeDtypeStruct((b, s, GQA_WIDTH), BF16)] + kv_shapes
        out_specs = [pl.BlockSpec((1, tile, DIFF_WIDTH), lambda bi, i: (bi, i, 0)),
                     pl.BlockSpec((1, tile, GQA_WIDTH), lambda bi, i: (bi, i, 0))] + kv_specs
    else:
        out_shapes, out_specs = kv_shapes, kv_specs
    const = lambda bi, i: (0, 0)
    return pl.pallas_call(
        functools.partial(_inproj_kernel, with_q=with_q, with_rope=with_q),
        grid=(b, nt),
        in_specs=[pl.BlockSpec((1, tile, d), lambda bi, i: (bi, i, 0)),
                  pl.BlockSpec((1, 6, d), mod_row),
                  pl.BlockSpec((1, d), const),
                  pl.BlockSpec((d, n), const),
                  pl.BlockSpec((tile, LANES), lambda bi, i: (i, 0)),
                  pl.BlockSpec((tile, LANES), lambda bi, i: (i, 0)),
                  pl.BlockSpec((tile, LANES), lambda bi, i: (i, 0)),
                  pl.BlockSpec(bd.shape, const),
                  pl.BlockSpec(gqg.shape, const),
                  pl.BlockSpec(gkg.shape, const)],
        out_specs=out_specs,
        out_shape=out_shapes,
        compiler_params=_cparams(("parallel", "parallel")),
        name="inproj_x" if with_q else "inproj_ctx",
    )(x, mod3, g1, w, cos, sa, sb, bd, gqg, gkg)


def _attn_steps(ks, qs, vts, carries):
    ss = [lax.dot_general(k, q, (((1,), (1,)), ((), ())), preferred_element_type=F32) for k, q in zip(ks, qs)]
    stats = []
    for s, (m, l, _) in zip(ss, carries):
        m_new = jnp.maximum(m, jnp.max(s, axis=0, keepdims=True))
        alpha = jnp.exp(m - m_new)
        p = jnp.exp(s - m_new)
        stats.append((m_new, alpha, alpha * l + jnp.sum(p, axis=0, keepdims=True), p.astype(BF16)))
    return tuple((m_new, l, alpha * acc + jnp.dot(vt, p, preferred_element_type=F32))
                 for (m_new, alpha, l, p), vt, (_, _, acc) in zip(stats, vts, carries))


def _attn_kernel(dq_ref, gq_ref, dkx_ref, gkx_ref, dvx_ref, gvx_ref, dkc_ref, gkc_ref, dvc_ref, gvc_ref,
                 lamv_ref, subg_ref, o_ref):
    tq = dq_ref.shape[1]
    n_chunks, chunk = dvx_ref.shape[1], dvx_ref.shape[3]
    lv = lamv_ref[...]
    lam = (jnp.exp(jnp.sum(lv[0:1] * lv[1:2], axis=-1, keepdims=True))
           - jnp.exp(jnp.sum(lv[2:3] * lv[3:4], axis=-1, keepdims=True)) + LAMBDA_INIT)
    low = lax.broadcasted_iota(I32, (tq, LANES), 1) < HEAD_DIM

    def run_units(units):
        def go(kx_ref, kc_ref, vx_ref, vc_ref, dv):
            qs = [q for q, _, _ in units]
            init = tuple((jnp.full((1, tq), -jnp.inf, F32), jnp.zeros((1, tq), F32), jnp.zeros((dv, tq), F32))
                         for _ in units)
            carries = _attn_steps([kc_ref[0, :, col:col + LANES] for _, col, _ in units], qs,
                                  [vc_ref[0, 0, r0:r0 + dv, :] for _, _, r0 in units], init)

            def body(c, carries):
                off = pl.multiple_of(c * chunk, chunk)
                return _attn_steps([kx_ref[0, pl.ds(off, chunk), col:col + LANES] for _, col, _ in units], qs,
                                   [vx_ref[0, c, r0:r0 + dv, :] for _, _, r0 in units], carries)

            return [acc / l for _, l, acc in lax.fori_loop(0, n_chunks, body, carries)]
        return go

    zero = jnp.zeros((tq, LANES), BF16)
    for h0 in range(0, DIFF_HEADS, ATTN_GROUP // 2):
        heads = range(h0, h0 + ATTN_GROUP // 2)
        units = []
        for h in heads:
            qh = dq_ref[0, :, h * LANES:(h + 1) * LANES]
            units += [(jnp.where(low, qh, zero), h * LANES, h * DIFF_V_DIM),
                      (jnp.where(low, zero, qh), h * LANES, h * DIFF_V_DIM)]
        outs = run_units(units)(dkx_ref, dkc_ref, dvx_ref, dvc_ref, DIFF_V_DIM)
        for n, h in enumerate(heads):
            od = outs[2 * n] - lam * outs[2 * n + 1]
            ms = jnp.mean(od * od, axis=0, keepdims=True)
            od = od * lax.rsqrt(ms + NORM_EPS) * subg_ref[...] * (1.0 - LAMBDA_INIT)
            o_ref[0, :, h * LANES:(h + 1) * LANES] = od.T.astype(BF16)
    for t0 in range(0, GQA_Q_HEADS // 2, ATTN_GROUP // 2):
        tiles = range(t0, t0 + ATTN_GROUP // 2)
        units = []
        for t in tiles:
            qt = gq_ref[0, :, t * LANES:(t + 1) * LANES]
            units += [(jnp.where(low, qt, zero), 0, 0), (jnp.where(low, zero, qt), 0, HEAD_DIM)]
        outs = run_units(units)(gkx_ref, gkc_ref, gvx_ref, gvc_ref, HEAD_DIM)
        for n, t in enumerate(tiles):
            og = jnp.concatenate([outs[2 * n], outs[2 * n + 1]], axis=0)
            o_ref[0, :, DIFF_WIDTH + t * LANES: DIFF_WIDTH + (t + 1) * LANES] = og.T.astype(BF16)


def _attn_call(dq, gq, dkx, gkx, dvx, gvx, dkc, gkc, dvc, gvc, lamv, subg):
    b, s, _ = dq.shape
    nq = s // Q_TILE
    full3 = lambda a: pl.BlockSpec((1,) + a.shape[1:], lambda bi, i: (bi, 0, 0))
    full4 = lambda a: pl.BlockSpec((1,) + a.shape[1:], lambda bi, i: (bi, 0, 0, 0))
    const = lambda bi, i: (0, 0)
    return pl.pallas_call(
        _attn_kernel,
        grid=(b, nq),
        in_specs=[pl.BlockSpec((1, Q_TILE, DIFF_WIDTH), lambda bi, i: (bi, i, 0)),
                  pl.BlockSpec((1, Q_TILE, GQA_WIDTH), lambda bi, i: (bi, i, 0)),
                  full3(dkx), full3(gkx), full4(dvx), full4(gvx),
                  full3(dkc), full3(gkc), full4(dvc), full4(gvc),
                  pl.BlockSpec(lamv.shape, const), pl.BlockSpec(subg.shape, const)],
        out_specs=pl.BlockSpec((1, Q_TILE, DIFF_WIDTH + GQA_WIDTH), lambda bi, i: (bi, i, 0)),
        out_shape=jax.ShapeDtypeStruct((b, s, DIFF_WIDTH + GQA_WIDTH), BF16),
        compiler_params=_cparams(("parallel", "arbitrary")),
        name="attn",
    )(dq, gq, dkx, gkx, dvx, gvx, dkc, gkc, dvc, gvc, lamv, subg)


def _out_kernel(o_ref, w_ref, x_ref, mod_ref, g2_ref, x1_ref, h2_ref):
    attn = jnp.dot(o_ref[0], w_ref[...], preferred_element_type=F32)
    x1 = x_ref[0] + mod_ref[0, 2:3, :] * attn
    x1_ref[0] = x1
    ms = jnp.mean(x1 * x1, axis=-1, keepdims=True)
    y = x1 * lax.rsqrt(ms + NORM_EPS) * g2_ref[...]
    h2_ref[0] = (y * (1.0 + mod_ref[0, 4:5, :]) + mod_ref[0, 3:4, :]).astype(BF16)


def _out_call(o, w_out, x, mod3, g2):
    b, s, d = x.shape
    const = lambda bi, i: (0, 0)
    tile = lambda w: pl.BlockSpec((1, OUT_TILE, w), lambda bi, i: (bi, i, 0))
    return pl.pallas_call(
        _out_kernel,
        grid=(b, s // OUT_TILE),
        in_specs=[tile(o.shape[2]), pl.BlockSpec(w_out.shape, const), tile(d),
                  pl.BlockSpec((1, 6, d), lambda bi, i: (bi, 0, 0)), pl.BlockSpec((1, d), const)],
        out_specs=[tile(d), tile(d)],
        out_shape=[jax.ShapeDtypeStruct((b, s, d), F32), jax.ShapeDtypeStruct((b, s, d), BF16)],
        compiler_params=_cparams(("parallel", "parallel")),
        name="outproj",
    )(o, w_out, x, mod3, g2)


def _topk_sublanes(s, flat, k):
    t = s.shape[1]
    big = jnp.iinfo(jnp.int32).max
    rowk = lax.broadcasted_iota(I32, (k, t), 0)
    vals = jnp.zeros((k, t), F32)
    labs = jnp.zeros((k, t), I32)
    for r in range(k):
        m = jnp.max(s, axis=0, keepdims=True)
        lab = jnp.min(jnp.where(s == m, flat, big), axis=0, keepdims=True)
        vals = jnp.where(rowk == r, m, vals)
        labs = jnp.where(rowk == r, lab, labs)
        s = jnp.where(flat == lab, -jnp.inf, s)
    return vals, labs


def _select_head(h, qt_ref, sk_ref, gates_t_ref, idst_ref):
    ts = qt_ref.shape[1]
    k = PEER_TOPK
    key_iota = lax.broadcasted_iota(I32, (PEER_N_KEYS, ts), 0)
    keep = k // 2
    sub = lax.broadcasted_iota(I32, (keep, ts), 0)
    flat_rows = [lax.broadcasted_iota(I32, (k, ts), 0)] + [i * k + sub for i in range(1, k)]
    cand_flat = jnp.concatenate(flat_rows, axis=0)
    r0 = pl.multiple_of(h * 2 * PEER_HALF, 2 * PEER_HALF)
    tops = []
    for p in range(2):
        q = qt_ref[pl.ds(r0 + p * PEER_HALF, PEER_HALF), :]
        s = jnp.dot(sk_ref[2 * h + p], q, preferred_element_type=F32)
        tops.append(_topk_sublanes(s, key_iota, k))
    (v1, i1), (v2, i2) = tops
    vals = [v1[0:1] + v2] + [v1[i:i + 1] + v2[:keep] for i in range(1, k)]
    eids = [i1[0:1] * PEER_N_KEYS + i2] + [i1[i:i + 1] * PEER_N_KEYS + i2[:keep] for i in range(1, k)]
    cand = jnp.concatenate(vals, axis=0)
    cid = jnp.concatenate(eids, axis=0)
    best, pos = _topk_sublanes(cand, cand_flat, k)
    rowk = lax.broadcasted_iota(I32, (k, ts), 0)
    ids = jnp.zeros((k, ts), I32)
    for r in range(k):
        pick = jnp.sum(jnp.where(cand_flat == pos[r:r + 1], cid, 0), axis=0, keepdims=True)
        ids = jnp.where(rowk == r, pick, ids)
    e = jnp.exp(best - best[0:1])
    o0 = pl.multiple_of(h * k, k)
    gates_t_ref[pl.ds(o0, k), :] = e / jnp.sum(e, axis=0, keepdims=True)
    idst_ref[pl.ds(o0, k), :] = ids


def _peer_kernel(h_cur_ref, h_sel_ref, h_s0_ref, h_s1_ref, x1_ref, mod_ref, fg_ref, wqt_ref, sk_ref, uv_hbm, o_ref,
                 buf0, buf1, sem, ids_s, gates_s, qt_ref, idst_ref, idtok_ref, gates_t_ref, mix_ref, sem_s,
                 *, nsteps):
    tt = PEER_TILE
    nsel = PEER_HEADS * PEER_TOPK
    d = h_cur_ref.shape[1]
    npair = PEER_GROUP // 2
    i = pl.program_id(0)
    bufs = (buf0, buf1)
    r_cur, r_next, r_sel = lax.rem(i, 3), lax.rem(i + 1, 3), lax.rem(i + 2, 3)

    def row_copy(ring, tok, t, e, sl):
        return pltpu.make_async_copy(uv_hbm.at[ids_s[ring, tok, e]],
                                     bufs[sl].at[pl.ds((t * nsel + e) * SUBLANES, SUBLANES)], sem.at[sl])

    def issue(ring, tok0, sl):
        for t in range(tt):
            for e in range(nsel):
                row_copy(ring, tok0 + t, t, e, sl).start(priority=e % 2)

    def wait_slot(sl):
        pltpu.make_async_copy(bufs[1 - sl], bufs[sl], sem.at[sl]).wait()

    def mix(sl, k):
        buf = bufs[sl]
        t0 = pl.multiple_of(k * tt, tt)
        hx = h_cur_ref[pl.ds(t0, tt), :].astype(F32)
        lane_t = lax.broadcasted_iota(I32, (nsel, tt), 1)

        def rows(t):
            base = t * nsel * SUBLANES
            return jnp.concatenate([buf[pl.ds(base + c, nsel, stride=SUBLANES), :] for c in range(SUBLANES)],
                                   axis=1)

        a = jnp.zeros((nsel, tt), F32)
        for t in range(tt):
            u = pltpu.bitcast(rows(t) << HALF_BITS, F32)
            prod = u * hx[t:t + 1, :]
            part = prod[:, 0:LANES]
            for c in range(1, d // LANES):
                part = part + prod[:, c * LANES:(c + 1) * LANES]
            a = jnp.where(lane_t == t, jnp.sum(part, axis=1, keepdims=True), a)
        gelu = 0.5 * a * (1.0 + lax.erf(a * (2.0 ** -0.5)))
        w = gates_s[r_cur, k] * gelu
        hi_mask = jnp.uint32(HIGH_HALF)
        for t in range(tt):
            v = pltpu.bitcast(rows(t) & hi_mask, F32)
            mix_ref[pl.ds(t0 + t, 1), :] = jnp.sum(v * w[:, t:t + 1], axis=0, keepdims=True)

    def project(h_ref):
        qt_ref[...] = lax.dot_general(wqt_ref[...], h_ref[...], (((1,), (1,)), ((), ())),
                                      preferred_element_type=F32).astype(BF16)

    def publish(ring):
        idtok_ref[...] = idst_ref[...].T
        cp = pltpu.make_async_copy(idtok_ref, ids_s.at[ring], sem_s.at[0])
        cp.start()
        cp.wait()
        for k in range(PEER_GROUP):
            gates_s[ring, k] = gates_t_ref[:, k * tt:(k + 1) * tt]

    def select_all(h_ref, ring):
        project(h_ref)

        def head(h, c):
            _select_head(h, qt_ref, sk_ref, gates_t_ref, idst_ref)
            return c
        lax.fori_loop(0, PEER_HEADS, head, 0)
        publish(ring)

    @pl.when(i == 0)
    def _():
        select_all(h_s0_ref, 0)
        select_all(h_s1_ref, 1)

        def body(t, c):
            for e in range(nsel):
                row_copy(0, t, t, e, 0).start(priority=e % 2)
            return c
        lax.fori_loop(0, tt, body, 0)

    heads_per_tile = PEER_HEADS // PEER_GROUP

    def select_heads(k):
        for j in range(heads_per_tile):
            _select_head(k * heads_per_tile + j, qt_ref, sk_ref, gates_t_ref, idst_ref)

    def pair(p, c):
        k0 = 2 * p
        last = p == npair - 1

        @pl.when(p == 0)
        def _():
            project(h_sel_ref)

        wait_slot(0)
        issue(r_cur, (k0 + 1) * tt, 1)
        mix(0, k0)
        select_heads(k0)
        wait_slot(1)
        issue(jnp.where(last, r_next, r_cur), jnp.where(last, 0, (k0 + 2) * tt), 0)
        mix(1, k0 + 1)
        select_heads(k0 + 1)

        @pl.when(last)
        def _():
            publish(r_sel)
        return c

    lax.fori_loop(0, npair, pair, 0)

    y = x1_ref[...] + mod_ref[0, 5:6, :] * mix_ref[...]
    ms = jnp.mean(y * y, axis=-1, keepdims=True)
    o_ref[...] = y * lax.rsqrt(ms + NORM_EPS) * fg_ref[...]

    @pl.when(i == nsteps - 1)
    def _():
        wait_slot(0)


def _peer_call(h2, x1, mod3, fg, wqt, sk, uv, seq):
    t, d = h2.shape
    nsel = PEER_HEADS * PEER_TOPK
    tt = PEER_TILE
    rows_per_step = PEER_GROUP * tt
    assert rows_per_step == SEL_TILE and seq % rows_per_step == 0
    assert PEER_GROUP % 2 == 0 and PEER_HEADS % PEER_GROUP == 0
    nsteps = t // rows_per_step
    assert nsteps >= 2
    per_batch = seq // rows_per_step
    rows = lambda f: pl.BlockSpec((rows_per_step, d), f)
    return pl.pallas_call(
        functools.partial(_peer_kernel, nsteps=nsteps),
        grid=(nsteps,),
        in_specs=[rows(lambda i: (i, 0)),
                  rows(lambda i: (jnp.minimum(i + 2, nsteps - 1), 0)),
                  rows(lambda i: (0, 0)),
                  rows(lambda i: (1, 0)),
                  rows(lambda i: (i, 0)),
                  pl.BlockSpec((1, 6, d), lambda i: (i // per_batch, 0, 0)),
                  pl.BlockSpec((1, d), lambda i: (0, 0)),
                  pl.BlockSpec(wqt.shape, lambda i: (0, 0)),
                  pl.BlockSpec(sk.shape, lambda i: (0, 0, 0)),
                  pl.BlockSpec(memory_space=pl.ANY)],
        out_specs=rows(lambda i: (i, 0)),
        out_shape=jax.ShapeDtypeStruct((t, d), F32),
        scratch_shapes=[pltpu.VMEM((tt * nsel * SUBLANES, LANES), jnp.uint32),
                        pltpu.VMEM((tt * nsel * SUBLANES, LANES), jnp.uint32),
                        pltpu.SemaphoreType.DMA((2,)),
                        pltpu.SMEM((3, rows_per_step, nsel), I32),
                        pltpu.VMEM((3, PEER_GROUP, nsel, tt), F32),
                        pltpu.VMEM((wqt.shape[0], rows_per_step), BF16),
                        pltpu.VMEM((nsel, rows_per_step), I32),
                        pltpu.VMEM((rows_per_step, nsel), I32),
                        pltpu.VMEM((nsel, rows_per_step), F32),
                        pltpu.VMEM((rows_per_step, d), F32),
                        pltpu.SemaphoreType.DMA((1,))],
        compiler_params=_cparams(("arbitrary",)),
        name="peer",
    )(h2, h2, h2, h2, x1, mod3, fg, wqt, sk, uv)


def _rope_tables(seq):
    half = HEAD_DIM // 4
    freqs = ROPE_THETA ** (-jnp.arange(half, dtype=F32) / half)
    pos = jnp.arange(seq, dtype=jnp.int32)
    ang_r = (pos // GRID_W).astype(F32)[:, None] * freqs[None, :]
    ang_c = (pos % GRID_W).astype(F32)[:, None] * freqs[None, :]
    z = jnp.zeros_like(ang_r)
    cos = jnp.concatenate([jnp.cos(ang_r)] * 2 + [jnp.cos(ang_c)] * 2, axis=1)
    sa = jnp.concatenate([-jnp.sin(ang_r), z, -jnp.sin(ang_c), z], axis=1)
    sb = jnp.concatenate([z, jnp.sin(ang_r), z, jnp.sin(ang_c)], axis=1)
    rep = LANES // HEAD_DIM
    return tuple(jnp.tile(a, (1, rep)) for a in (cos, sa, sb))


def kernel(x, c, ctx, c_ctx, w_mod, b_mod, norm1_g, norm2_g, w_in, w_out, diff_lq1, diff_lk1, diff_lq2,
           diff_lk2, diff_subln_g, gqa_q_norm_g, gqa_k_norm_g, peer_wq, peer_subkeys, peer_u, peer_v,
           final_norm_g):
    b, s, d = x.shape
    assert w_mod.shape[0] == 1, "depth-1 block"
    assert s % IN_TILE == 0 and s % Q_TILE == 0 and ctx.shape[1] % LANES == 0

    pad = (-(b + 1)) % 8
    cc = jnp.concatenate([c, c_ctx[None, :], jnp.zeros((pad, d), F32)], axis=0)
    mod3 = _mod_call(cc, w_mod[0], b_mod[0][None, :]).reshape(cc.shape[0], 6, d)

    w0 = w_in[0]
    o_dk, o_dv, o_gq, o_gk, o_gv = (DIFF_WIDTH, 2 * DIFF_WIDTH, 3 * DIFF_WIDTH, 3 * DIFF_WIDTH + GQA_WIDTH,
                                    3 * DIFF_WIDTH + GQA_WIDTH + GQA_KV_WIDTH)
    group = GQA_Q_HEADS // GQA_KV_HEADS
    head_order = [j * group + t for t in range(group) for j in range(GQA_KV_HEADS)]
    gq_cols = jnp.asarray([hd * HEAD_DIM + e for hd in head_order for e in range(HEAD_DIM)], dtype=jnp.int32)
    w_gq = w0[:, o_gq:o_gk][:, gq_cols]
    w_x = jnp.concatenate([w0[:, :o_gq], w_gq, w0[:, o_gk:]], axis=1).astype(BF16)
    w_c = jnp.concatenate([w0[:, o_dk:o_gq], w0[:, o_gk:]], axis=1).astype(BF16)
    w_o = jnp.concatenate([w_out[0][:DIFF_WIDTH], w_out[0][DIFF_WIDTH:][gq_cols]], axis=0).astype(BF16)

    seg = jnp.arange(GQA_WIDTH) // HEAD_DIM
    bd = (seg[:, None] == seg[None, :]).astype(BF16)
    gqg = jnp.tile(gqa_q_norm_g[0], GQA_Q_HEADS)[None, :]
    gkg = jnp.tile(gqa_k_norm_g[0], GQA_KV_HEADS)[None, :]
    g1 = norm1_g[0][None, :]
    tables = _rope_tables(s)

    dq, gq, dkx, gkx, dvx, gvx = _inproj_call(x, mod3, lambda bi, i: (bi, 0, 0), g1, w_x, tables, bd, gqg, gkg,
                                              with_q=True, tile=IN_TILE)
    ctx_tile = ctx.shape[1]
    ctx_tables = tuple(a[:ctx_tile] for a in tables)
    dkc, gkc, dvc, gvc = _inproj_call(ctx, mod3, lambda bi, i: (b, 0, 0), g1, w_c, ctx_tables, bd, gqg, gkg,
                                      with_q=False, tile=ctx_tile)

    lamv = jnp.stack([diff_lq1[0], diff_lk1[0], diff_lq2[0], diff_lk2[0]], axis=0).astype(F32)
    subg = diff_subln_g[0][:, None]
    o = _attn_call(dq, gq, dkx, gkx, dvx, gvx, dkc, gkc, dvc, gvc, lamv, subg)

    x1, h2 = _out_call(o, w_o, x, mod3, norm2_g[0][None, :])

    t = b * s
    h2f = h2.reshape(t, d)
    wqt = peer_wq[0].T.astype(BF16)
    sk = peer_subkeys[0].reshape(PEER_HEADS * 2, PEER_N_KEYS, PEER_HALF).astype(BF16)
    half_bits = lambda a: lax.bitcast_convert_type(a.astype(BF16), jnp.uint16).astype(jnp.uint32)
    uv = (half_bits(peer_u[0]) | (half_bits(peer_v[0]) << HALF_BITS)).reshape(-1, SUBLANES, LANES)
    assert d == SUBLANES * LANES
    out = _peer_call(h2f, x1.reshape(t, d), mod3, final_norm_g[None, :], wqt, sk, uv, s)
    return out.reshape(b, s, d)
```
